```python
import math
import jax, jax.numpy as jnp
from jax import lax
import numpy as np

D_MODEL = 1024
BATCH = 16
SEQ = 2048
DEPTH = 1

HEAD_DIM = 64
N_Q_HEADS = 8
N_KV_HEADS = 2
Q_PER_KV = N_Q_HEADS // N_KV_HEADS
ATTN_WIDTH = N_Q_HEADS * HEAD_DIM
KV_WIDTH = N_KV_HEADS * HEAD_DIM
N_FOURIER_GROUPS = 8
FOURIER_GROUP_DIM = 64
FOURIER_WIDTH = N_FOURIER_GROUPS * FOURIER_GROUP_DIM
MIX_WIDTH = ATTN_WIDTH + FOURIER_WIDTH
IN_PROJ_WIDTH = ATTN_WIDTH + 2 * KV_WIDTH + FOURIER_WIDTH
D_FF = 4 * D_MODEL
GRID_W = 64
AXIS_DIM = HEAD_DIM // 2
ROPE_THETA = 10000.0
Q_BLOCK = 128
NORM_EPS = 1e-6

kernel_name = "hymba_style_fnet_axial_gqa_block"


def rms_norm(x, g):
    xf = x.astype(jnp.float32)
    y = xf * lax.rsqrt(jnp.mean(xf * xf, axis=-1, keepdims=True) + NORM_EPS)
    return (y * g.astype(jnp.float32)).astype(x.dtype)


def axial_angles(seq_len):
    rows = seq_len // GRID_W
    row = jnp.repeat(jnp.arange(rows, dtype=jnp.int32), GRID_W)
    col = jnp.tile(jnp.arange(GRID_W, dtype=jnp.int32), rows)
    inv_freq = ROPE_THETA ** (-jnp.arange(0, AXIS_DIM, 2, dtype=jnp.float32) / AXIS_DIM)
    row_ang = row.astype(jnp.float32)[:, None] * inv_freq[None, :]
    col_ang = col.astype(jnp.float32)[:, None] * inv_freq[None, :]
    return row_ang, col_ang


def rotate_half_axis(x, ang):
    half = AXIS_DIM // 2
    c = jnp.cos(ang).astype(x.dtype)
    s = jnp.sin(ang).astype(x.dtype)
    x1, x2 = x[..., :half], x[..., half:]
    return jnp.concatenate([x1 * c - x2 * s, x1 * s + x2 * c], axis=-1)


def apply_axial_rope(x, row_ang, col_ang):
    return jnp.concatenate([rotate_half_axis(x[..., :AXIS_DIM], row_ang),
                            rotate_half_axis(x[..., AXIS_DIM:], col_ang)], axis=-1)


def gqa_axial_attention(q, k, v, q_norm_g, k_norm_g):
    B, S, _ = q.shape
    q = q.reshape(B, S, N_Q_HEADS, HEAD_DIM)
    k = k.reshape(B, S, N_KV_HEADS, HEAD_DIM)
    v = v.reshape(B, S, N_KV_HEADS, HEAD_DIM)
    q = rms_norm(q, q_norm_g).transpose(0, 2, 1, 3)
    k = rms_norm(k, k_norm_g).transpose(0, 2, 1, 3)
    v = v.transpose(0, 2, 1, 3)
    row_ang, col_ang = axial_angles(S)
    q = apply_axial_rope(q, row_ang, col_ang)
    k = apply_axial_rope(k, row_ang, col_ang)
    q = q * jnp.asarray(HEAD_DIM ** -0.5, dtype=q.dtype)
    n_blk = S // Q_BLOCK
    qb = q.reshape(B, N_KV_HEADS, Q_PER_KV, n_blk, Q_BLOCK, HEAD_DIM)
    qb = jnp.moveaxis(qb, 3, 0)

    def one_block(q_blk):
        s = jnp.einsum('bkgqd,bksd->bkgqs', q_blk, k).astype(jnp.float32)
        p = jax.nn.softmax(s, axis=-1).astype(v.dtype)
        return jnp.einsum('bkgqs,bksd->bkgqd', p, v)

    o = lax.map(one_block, qb)
    o = o.transpose(1, 0, 4, 2, 3, 5)
    return o.reshape(B, S, ATTN_WIDTH)


def fourier_mixer(u, w_fourier):
    B, S, _ = u.shape
    ug = u.reshape(B, S, N_FOURIER_GROUPS, FOURIER_GROUP_DIM).astype(jnp.float32)
    f = jnp.fft.fftn(ug, axes=(1, 3), norm='ortho').real.astype(u.dtype)
    y = jnp.einsum('bsgc,gcd->bsgd', f, w_fourier)
    return y.reshape(B, S, FOURIER_WIDTH)


def setup_inputs(seed: int = 0) -> dict:
    key = jax.random.key(seed)
    ks = jax.random.split(key, 12)
    f32 = jnp.float32
    x = jax.random.normal(ks[0], (BATCH, SEQ, D_MODEL), f32)
    mix_norm_g = 1.0 + 0.05 * jax.random.normal(ks[1], (D_MODEL,), f32)
    w_in = jax.random.normal(ks[2], (D_MODEL, IN_PROJ_WIDTH), f32) * D_MODEL ** -0.5
    q_norm_g = 1.0 + 0.05 * jax.random.normal(ks[3], (HEAD_DIM,), f32)
    k_norm_g = 1.0 + 0.05 * jax.random.normal(ks[4], (HEAD_DIM,), f32)
    w_fourier = jax.random.normal(ks[5], (N_FOURIER_GROUPS, FOURIER_GROUP_DIM, FOURIER_GROUP_DIM), f32) * FOURIER_GROUP_DIM ** -0.5
    w_out = jax.random.normal(ks[6], (MIX_WIDTH, D_MODEL), f32) * MIX_WIDTH ** -0.5
    mlp_norm_g = 1.0 + 0.05 * jax.random.normal(ks[7], (D_MODEL,), f32)
    w_up = jax.random.normal(ks[8], (D_MODEL, D_FF), f32) * D_MODEL ** -0.5
    w_down = jax.random.normal(ks[9], (D_FF, D_MODEL), f32) * D_FF ** -0.5
    final_norm_g = 1.0 + 0.05 * jax.random.normal(ks[10], (D_MODEL,), f32)
    return {"x": x, "mix_norm_g": mix_norm_g, "w_in": w_in, "q_norm_g": q_norm_g,
            "k_norm_g": k_norm_g, "w_fourier": w_fourier, "w_out": w_out,
            "mlp_norm_g": mlp_norm_g, "w_up": w_up, "w_down": w_down,
            "final_norm_g": final_norm_g}


def reference(x, mix_norm_g, w_in, q_norm_g, k_norm_g, w_fourier, w_out,
              mlp_norm_g, w_up, w_down, final_norm_g):
    for _ in range(DEPTH):
        h = rms_norm(x, mix_norm_g)
        proj = jnp.einsum('bsd,de->bse', h, w_in)
        q = proj[..., :ATTN_WIDTH]
        k = proj[..., ATTN_WIDTH:ATTN_WIDTH + KV_WIDTH]
        v = proj[..., ATTN_WIDTH + KV_WIDTH:ATTN_WIDTH + 2 * KV_WIDTH]
        u = proj[..., ATTN_WIDTH + 2 * KV_WIDTH:]
        attn_out = gqa_axial_attention(q, k, v, q_norm_g, k_norm_g)
        four_out = fourier_mixer(u, w_fourier)
        mixed = jnp.concatenate([attn_out, four_out], axis=-1)
        x = x + jnp.einsum('bse,ed->bsd', mixed, w_out)
        h = rms_norm(x, mlp_norm_g)
        z = jnp.einsum('bsd,df->bsf', h, w_up)
        z = jnp.square(jax.nn.relu(z))
        x = x + jnp.einsum('bsf,fd->bsd', z, w_down)
    return rms_norm(x, final_norm_g)
```

```python
import functools

import numpy as np
import jax
import jax.numpy as jnp
from jax import lax
from jax.experimental import pallas as pl
from jax.experimental.pallas import tpu as pltpu

D_MODEL = 1024
HEAD_DIM = 64
N_Q_HEADS = 8
N_KV_HEADS = 2
ATTN_WIDTH = N_Q_HEADS * HEAD_DIM
KV_WIDTH = N_KV_HEADS * HEAD_DIM
N_GROUPS = 8
GROUP_DIM = 64
FOURIER_WIDTH = N_GROUPS * GROUP_DIM
IN_PROJ_WIDTH = ATTN_WIDTH + 2 * KV_WIDTH + FOURIER_WIDTH
D_FF = 4 * D_MODEL
GRID_W = 64
AXIS_DIM = HEAD_DIM // 2
ROPE_THETA = 10000.0
NORM_EPS = 1e-6

V7X_LANES = 128
V7X_VMEM_BYTES = 64 * 1024 * 1024

TOKEN_TILE = 512
Q_TILE = 256
FF_CHUNK = 1024
VMEM_LIMIT = 56 * 1024 * 1024

BF16 = jnp.bfloat16
F32 = jnp.float32


def _resident(shape):
    zeros = (0,) * len(shape)
    return pl.BlockSpec(shape, lambda *_: zeros, pipeline_mode=pl.Buffered(1))


def _rms_scale(x):
    return lax.rsqrt(jnp.mean(x * x, axis=-1, keepdims=True) + NORM_EPS)


def _in_proj_kernel(x_ref, g_ref, w_ref, ones_ref, qg_ref, kg_ref, cos_ref, sa_ref,
                    sb_ref, q_ref, kt_ref, vx_ref, u_ref):
    x = x_ref[...]
    h = x * _rms_scale(x) * g_ref[...]
    proj = jnp.dot(h.astype(BF16), w_ref[...], preferred_element_type=F32)

    cos = cos_ref[...]
    sin_a = sa_ref[...]
    sin_b = sb_ref[...]
    ones = ones_ref[...]

    def norm_rope(blk, gain, scale):
        ss = jnp.dot((blk * blk).astype(BF16), ones, preferred_element_type=F32)
        y = blk * (lax.rsqrt(ss * (1.0 / HEAD_DIM) + NORM_EPS) * scale) * gain
        return (y * cos + pltpu.roll(y, V7X_LANES - 16, 1) * sin_a
                + pltpu.roll(y, 16, 1) * sin_b)

    qg = qg_ref[...]
    for c in range(ATTN_WIDTH // V7X_LANES):
        sl = slice(c * V7X_LANES, (c + 1) * V7X_LANES)
        q_ref[:, sl] = norm_rope(proj[:, sl], qg, HEAD_DIM ** -0.5).astype(BF16)

    k = norm_rope(proj[:, ATTN_WIDTH:ATTN_WIDTH + KV_WIDTH], kg_ref[...], 1.0)
    kt = k.T.astype(BF16)
    top, bot = kt[:HEAD_DIM], kt[HEAD_DIM:]
    zero = jnp.zeros_like(top)
    for v, (first, second) in enumerate(((top, zero), (zero, top), (bot, zero), (zero, bot))):
        kt_ref[0, v * V7X_LANES:v * V7X_LANES + HEAD_DIM, :] = first
        kt_ref[0, v * V7X_LANES + HEAD_DIM:(v + 1) * V7X_LANES, :] = second

    vv = proj[:, ATTN_WIDTH + KV_WIDTH:ATTN_WIDTH + 2 * KV_WIDTH]
    vr = pltpu.roll(vv, HEAD_DIM, 1)
    lo = lax.broadcasted_iota(jnp.int32, vv.shape, 1) < HEAD_DIM
    for v, blk in enumerate((jnp.where(lo, vv, 1.0), jnp.where(lo, 1.0, vr),
                             jnp.where(lo, vr, 1.0), jnp.where(lo, 1.0, vv))):
        vx_ref[:, v * V7X_LANES:(v + 1) * V7X_LANES] = blk.astype(BF16)

    u_ref[...] = proj[:, ATTN_WIDTH + 2 * KV_WIDTH:].astype(BF16)


def _in_proj(x2, mix_g, w_in, ones_bd, qg, kg, cos_t, sin_a, sin_b, batch, seq):
    n = x2.shape[0]
    tiles_per_seq = seq // TOKEN_TILE
    tok = lambda i: (i, 0)
    pos = lambda i: (i % tiles_per_seq, 0)
    return pl.pallas_call(
        _in_proj_kernel,
        grid=(n // TOKEN_TILE,),
        in_specs=[
            pl.BlockSpec((TOKEN_TILE, D_MODEL), tok),
            _resident((1, D_MODEL)),
            _resident((D_MODEL, IN_PROJ_WIDTH)),
            _resident((V7X_LANES, V7X_LANES)),
            _resident((1, V7X_LANES)),
            _resident((1, V7X_LANES)),
            pl.BlockSpec((TOKEN_TILE, V7X_LANES), pos),
            pl.BlockSpec((TOKEN_TILE, V7X_LANES), pos),
            pl.BlockSpec((TOKEN_TILE, V7X_LANES), pos),
        ],
        out_specs=[
            pl.BlockSpec((TOKEN_TILE, ATTN_WIDTH), tok),
            pl.BlockSpec((1, 4 * V7X_LANES, TOKEN_TILE),
                         lambda i: (i // tiles_per_seq, 0, i % tiles_per_seq)),
            pl.BlockSpec((TOKEN_TILE, 4 * V7X_LANES), tok),
            pl.BlockSpec((TOKEN_TILE, FOURIER_WIDTH), tok),
        ],
        out_shape=[
            jax.ShapeDtypeStruct((n, ATTN_WIDTH), BF16),
            jax.ShapeDtypeStruct((batch, 4 * V7X_LANES, seq), BF16),
            jax.ShapeDtypeStruct((n, 4 * V7X_LANES), BF16),
            jax.ShapeDtypeStruct((n, FOURIER_WIDTH), BF16),
        ],
        compiler_params=pltpu.CompilerParams(
            dimension_semantics=("arbitrary",), vmem_limit_bytes=VMEM_LIMIT),
        name="in_proj",
    )(x2, mix_g, w_in, ones_bd, qg, kg, cos_t, sin_a, sin_b)


def _attn_kernel(q_ref, kt_ref, vx_ref, o_ref):
    lo = lax.broadcasted_iota(jnp.int32, (Q_TILE, V7X_LANES), 1) < HEAD_DIM
    for c in range(ATTN_WIDTH // V7X_LANES):
        qc = q_ref[:, c * V7X_LANES:(c + 1) * V7X_LANES]
        halves = []
        for j in range(2):
            v = 2 * ((2 * c + j) // (N_Q_HEADS // N_KV_HEADS)) + j
            vs = slice(v * V7X_LANES, (v + 1) * V7X_LANES)
            s = jnp.dot(qc, kt_ref[0, vs, :], preferred_element_type=F32)
            m = jnp.max(s, axis=-1, keepdims=True)
            e = jnp.exp(s - m).astype(BF16)
            r = jnp.dot(e, vx_ref[:, vs], preferred_element_type=F32)
            halves.append(r / pltpu.roll(r, HEAD_DIM, 1))
        o_ref[:, c * V7X_LANES:(c + 1) * V7X_LANES] = (
            jnp.where(lo, halves[0], halves[1]).astype(BF16))


def _attention(q, kt, vx, batch, seq):
    nq = seq // Q_TILE
    return pl.pallas_call(
        _attn_kernel,
        grid=(batch, nq),
        in_specs=[
            pl.BlockSpec((Q_TILE, ATTN_WIDTH), lambda b, i: (b * nq + i, 0)),
            pl.BlockSpec((1, 4 * V7X_LANES, seq), lambda b, i: (b, 0, 0)),
            pl.BlockSpec((seq, 4 * V7X_LANES), lambda b, i: (b, 0)),
        ],
        out_specs=pl.BlockSpec((Q_TILE, ATTN_WIDTH), lambda b, i: (b * nq + i, 0)),
        out_shape=jax.ShapeDtypeStruct((batch * seq, ATTN_WIDTH), BF16),
        compiler_params=pltpu.CompilerParams(
            dimension_semantics=("arbitrary", "arbitrary"), vmem_limit_bytes=VMEM_LIMIT),
        name="attention",
    )(q, kt, vx)


def _fourier_kernel(u_ref, dft_ref, cc_ref, sc_ref, wbd_ref, y_ref, ma_ref, mb_ref):
    @pl.when(pl.program_id(0) == 0)
    def _():
        w = wbd_ref[...]
        ma_ref[...] = jnp.dot(cc_ref[...], w, precision=lax.Precision.HIGHEST,
                              preferred_element_type=F32).astype(BF16)
        mb_ref[...] = jnp.dot(sc_ref[...], w, precision=lax.Precision.HIGHEST,
                              preferred_element_type=F32).astype(BF16)

    u = u_ref[...]
    a = jnp.dot(u, ma_ref[...], preferred_element_type=F32).astype(BF16)
    b = jnp.dot(u, mb_ref[...], preferred_element_type=F32).astype(BF16)
    ab = jnp.concatenate([a, b], axis=0)
    y_ref[...] = jnp.dot(dft_ref[...], ab, preferred_element_type=F32).astype(BF16)


def _fourier(u, dft, cc, sc, wbd, batch, seq):
    return pl.pallas_call(
        _fourier_kernel,
        grid=(batch,),
        in_specs=[
            pl.BlockSpec((seq, FOURIER_WIDTH), lambda b: (b, 0)),
            _resident((seq, 2 * seq)),
            _resident((FOURIER_WIDTH, FOURIER_WIDTH)),
            _resident((FOURIER_WIDTH, FOURIER_WIDTH)),
            _resident((FOURIER_WIDTH, FOURIER_WIDTH)),
        ],
        out_specs=pl.BlockSpec((seq, FOURIER_WIDTH), lambda b: (b, 0)),
        out_shape=jax.ShapeDtypeStruct((batch * seq, FOURIER_WIDTH), BF16),
        scratch_shapes=[pltpu.VMEM((FOURIER_WIDTH, FOURIER_WIDTH), BF16),
                        pltpu.VMEM((FOURIER_WIDTH, FOURIER_WIDTH), BF16)],
        compiler_params=pltpu.CompilerParams(
            dimension_semantics=("arbitrary",), vmem_limit_bytes=VMEM_LIMIT),
        name="fourier",
    )(u, dft, cc, sc, wbd)


def _out_mlp_kernel(a_ref, f_ref, x_ref, wo_ref, g2_ref, wu_ref, wd_ref, g3_ref, o_ref):
    mix = (jnp.dot(a_ref[...], wo_ref[:ATTN_WIDTH, :], preferred_element_type=F32)
           + jnp.dot(f_ref[...], wo_ref[ATTN_WIDTH:, :], preferred_element_type=F32))
    x1 = x_ref[...] + mix
    h = (x1 * _rms_scale(x1) * g2_ref[...]).astype(BF16)
    acc = x1
    for c in range(D_FF // FF_CHUNK):
        cs = slice(c * FF_CHUNK, (c + 1) * FF_CHUNK)
        z = jnp.dot(h, wu_ref[:, cs], preferred_element_type=F32)
        z = jnp.square(jnp.maximum(z, 0.0)).astype(BF16)
        acc = acc + jnp.dot(z, wd_ref[cs, :], preferred_element_type=F32)
    o_ref[...] = acc * _rms_scale(acc) * g3_ref[...]


def _out_mlp(attn, four, x2, w_out, g2, w_up, w_down, g3):
    n = x2.shape[0]
    tok = lambda i: (i, 0)
    return pl.pallas_call(
        _out_mlp_kernel,
        grid=(n // TOKEN_TILE,),
        in_specs=[
            pl.BlockSpec((TOKEN_TILE, ATTN_WIDTH), tok),
            pl.BlockSpec((TOKEN_TILE, FOURIER_WIDTH), tok),
            pl.BlockSpec((TOKEN_TILE, D_MODEL), tok),
            _resident((D_MODEL, D_MODEL)),
            _resident((1, D_MODEL)),
            _resident((D_MODEL, D_FF)),
            _resident((D_FF, D_MODEL)),
            _resident((1, D_MODEL)),
        ],
        out_specs=pl.BlockSpec((TOKEN_TILE, D_MODEL), tok),
        out_shape=jax.ShapeDtypeStruct((n, D_MODEL), F32),
        compiler_params=pltpu.CompilerParams(
            dimension_semantics=("arbitrary",), vmem_limit_bytes=VMEM_LIMIT),
        name="out_mlp",
    )(attn, four, x2, w_out, g2, w_up, w_down, g3)


def _rope_tables(seq):
    rows = seq // GRID_W
    row = jnp.repeat(jnp.arange(rows, dtype=jnp.int32), GRID_W)
    col = jnp.tile(jnp.arange(GRID_W, dtype=jnp.int32), rows)
    inv_freq = ROPE_THETA ** (-jnp.arange(0, AXIS_DIM, 2, dtype=F32) / AXIS_DIM)
    row_ang = row.astype(F32)[:, None] * inv_freq[None, :]
    col_ang = col.astype(F32)[:, None] * inv_freq[None, :]
    ang = jnp.concatenate([row_ang, row_ang, col_ang, col_ang], axis=-1)
    ang = jnp.concatenate([ang, ang], axis=-1)
    first_half = (jnp.arange(V7X_LANES) % AXIS_DIM) < (AXIS_DIM // 2)
    sin = jnp.sin(ang)
    return (jnp.cos(ang), jnp.where(first_half, -sin, 0.0), jnp.where(first_half, 0.0, sin))


@functools.lru_cache(maxsize=None)
def _dft_constants(seq):
    j = np.arange(seq, dtype=np.int64)
    ang = 2.0 * np.pi * ((j[:, None] * j[None, :]) % seq) / seq
    dft = np.concatenate([np.cos(ang), -np.sin(ang)], axis=1)
    c = np.arange(GROUP_DIM, dtype=np.int64)
    angc = 2.0 * np.pi * ((c[:, None] * c[None, :]) % GROUP_DIM) / GROUP_DIM
    scale = 1.0 / np.sqrt(float(seq * GROUP_DIM))
    eye = np.eye(N_GROUPS)
    cc = np.kron(eye, np.cos(angc)) * scale
    sc = np.kron(eye, np.sin(angc)) * scale
    ones_bd = np.kron(np.eye(V7X_LANES // HEAD_DIM), np.ones((HEAD_DIM, HEAD_DIM)))
    return (dft.astype(np.float32), cc.astype(np.float32), sc.astype(np.float32),
            ones_bd.astype(np.float32))


def kernel(x, mix_norm_g, w_in, q_norm_g, k_norm_g, w_fourier, w_out, mlp_norm_g, w_up,
           w_down, final_norm_g):
    batch, seq, d_model = x.shape
    assert d_model == D_MODEL and seq % TOKEN_TILE == 0 and seq % Q_TILE == 0
    x2 = x.reshape(batch * seq, d_model)

    dft_np, cc_np, sc_np, ones_np = _dft_constants(seq)
    dft = jnp.asarray(dft_np).astype(BF16)
    cc = jnp.asarray(cc_np)
    sc = jnp.asarray(sc_np)
    ones_bd = jnp.asarray(ones_np, dtype=BF16)
    cos_t, sin_a, sin_b = _rope_tables(seq)

    row = lambda g: g.reshape(1, -1).astype(F32)
    two_heads = lambda g: jnp.tile(g.astype(F32), V7X_LANES // HEAD_DIM).reshape(1, V7X_LANES)
    wbd = (jnp.eye(N_GROUPS, dtype=F32)[:, None, :, None]
           * w_fourier.astype(F32)[:, :, None, :]).reshape(FOURIER_WIDTH, FOURIER_WIDTH)

    q, kt, vx, u = _in_proj(x2, row(mix_norm_g), w_in.astype(BF16), ones_bd,
                            two_heads(q_norm_g), two_heads(k_norm_g),
                            cos_t, sin_a, sin_b, batch, seq)
    attn = _attention(q, kt, vx, batch, seq)
    four = _fourier(u, dft, cc, sc, wbd, batch, seq)
    out = _out_mlp(attn, four, x2, w_out.astype(BF16), row(mlp_norm_g),
                   w_up.astype(BF16), w_down.astype(BF16), row(final_norm_g))
    return out.reshape(batch, seq, d_model)
```

```python
import functools

import numpy as np
import jax
import jax.numpy as jnp
from jax import lax
from jax.experimental import pallas as pl
from jax.experimental.pallas import tpu as pltpu

D_MODEL = 1024
HEAD_DIM = 64
N_Q_HEADS = 8
N_KV_HEADS = 2
ATTN_WIDTH = N_Q_HEADS * HEAD_DIM
KV_WIDTH = N_KV_HEADS * HEAD_DIM
N_GROUPS = 8
GROUP_DIM = 64
FOURIER_WIDTH = N_GROUPS * GROUP_DIM
IN_PROJ_WIDTH = ATTN_WIDTH + 2 * KV_WIDTH + FOURIER_WIDTH
D_FF = 4 * D_MODEL
GRID_W = 64
AXIS_DIM = HEAD_DIM // 2
ROPE_THETA = 10000.0
NORM_EPS = 1e-6
LOG2_E = 1.4426950408889634

V7X_LANES = 128
V7X_VMEM_BYTES = 64 * 1024 * 1024

TOKEN_TILE = 512
Q_TILE = 512
FF_CHUNK = 1024
VMEM_LIMIT = 56 * 1024 * 1024

BF16 = jnp.bfloat16
F32 = jnp.float32


def _resident(shape):
    zeros = (0,) * len(shape)
    return pl.BlockSpec(shape, lambda *_: zeros, pipeline_mode=pl.Buffered(1))


def _rms_scale(x):
    return lax.rsqrt(jnp.mean(x * x, axis=-1, keepdims=True) + NORM_EPS)


def _in_proj_kernel(x_ref, g_ref, w_ref, ones_ref, qg_ref, kg_ref, cos_ref, sa_ref,
                    sb_ref, qt_ref, kx_ref, vxt_ref, u_ref):
    x = x_ref[...]
    h = x * _rms_scale(x) * g_ref[...]
    proj = jnp.dot(h.astype(BF16), w_ref[...], preferred_element_type=F32)

    cos = cos_ref[...]
    sin_a = sa_ref[...]
    sin_b = sb_ref[...]
    ones = ones_ref[...]

    def norm_rope(blk, gain, scale):
        ss = jnp.dot((blk * blk).astype(BF16), ones, preferred_element_type=F32)
        y = blk * (lax.rsqrt(ss * (1.0 / HEAD_DIM) + NORM_EPS) * scale) * gain
        return (y * cos + pltpu.roll(y, V7X_LANES - 16, 1) * sin_a
                + pltpu.roll(y, 16, 1) * sin_b)

    qg = qg_ref[...]
    for c in range(ATTN_WIDTH // V7X_LANES):
        sl = slice(c * V7X_LANES, (c + 1) * V7X_LANES)
        qt_ref[0, sl, :] = norm_rope(proj[:, sl], qg, HEAD_DIM ** -0.5 * LOG2_E).T.astype(BF16)

    k = norm_rope(proj[:, ATTN_WIDTH:ATTN_WIDTH + KV_WIDTH], kg_ref[...], 1.0)
    kr = pltpu.roll(k, HEAD_DIM, 1)
    lo = lax.broadcasted_iota(jnp.int32, k.shape, 1) < HEAD_DIM
    for v, blk in enumerate((jnp.where(lo, k, 0.0), jnp.where(lo, 0.0, kr),
                             jnp.where(lo, kr, 0.0), jnp.where(lo, 0.0, k))):
        kx_ref[:, v * V7X_LANES:(v + 1) * V7X_LANES] = blk.astype(BF16)

    vt = proj[:, ATTN_WIDTH + KV_WIDTH:ATTN_WIDTH + 2 * KV_WIDTH].T.astype(BF16)
    top, bot = vt[:HEAD_DIM], vt[HEAD_DIM:]
    one = jnp.ones_like(top)
    for v, (first, second) in enumerate(((top, one), (one, top), (bot, one), (one, bot))):
        vxt_ref[0, v * V7X_LANES:v * V7X_LANES + HEAD_DIM, :] = first
        vxt_ref[0, v * V7X_LANES + HEAD_DIM:(v + 1) * V7X_LANES, :] = second

    u_ref[...] = proj[:, ATTN_WIDTH + 2 * KV_WIDTH:].astype(BF16)


def _in_proj(x2, mix_g, w_in, ones_bd, qg, kg, cos_t, sin_a, sin_b, batch, seq):
    n = x2.shape[0]
    tiles_per_seq = seq // TOKEN_TILE
    tok = lambda i: (i, 0)
    pos = lambda i: (i % tiles_per_seq, 0)
    chan_major = lambda i: (i // tiles_per_seq, 0, i % tiles_per_seq)
    return pl.pallas_call(
        _in_proj_kernel,
        grid=(n // TOKEN_TILE,),
        in_specs=[
            pl.BlockSpec((TOKEN_TILE, D_MODEL), tok),
            _resident((1, D_MODEL)),
            _resident((D_MODEL, IN_PROJ_WIDTH)),
            _resident((V7X_LANES, V7X_LANES)),
            _resident((1, V7X_LANES)),
            _resident((1, V7X_LANES)),
            pl.BlockSpec((TOKEN_TILE, V7X_LANES), pos),
            pl.BlockSpec((TOKEN_TILE, V7X_LANES), pos),
            pl.BlockSpec((TOKEN_TILE, V7X_LANES), pos),
        ],
        out_specs=[
            pl.BlockSpec((1, ATTN_WIDTH, TOKEN_TILE), chan_major),
            pl.BlockSpec((TOKEN_TILE, 4 * V7X_LANES), tok),
            pl.BlockSpec((1, 4 * V7X_LANES, TOKEN_TILE), chan_major),
            pl.BlockSpec((TOKEN_TILE, FOURIER_WIDTH), tok),
        ],
        out_shape=[
            jax.ShapeDtypeStruct((batch, ATTN_WIDTH, seq), BF16),
            jax.ShapeDtypeStruct((n, 4 * V7X_LANES), BF16),
            jax.ShapeDtypeStruct((batch, 4 * V7X_LANES, seq), BF16),
            jax.ShapeDtypeStruct((n, FOURIER_WIDTH), BF16),
        ],
        compiler_params=pltpu.CompilerParams(
            dimension_semantics=("arbitrary",), vmem_limit_bytes=VMEM_LIMIT),
        name="in_proj",
    )(x2, mix_g, w_in, ones_bd, qg, kg, cos_t, sin_a, sin_b)


def _attn_kernel(qt_ref, kx_ref, vxt_ref, o_ref):
    top = lax.broadcasted_iota(jnp.int32, (V7X_LANES, Q_TILE), 0) < HEAD_DIM

    def variant(h):
        v = 2 * (h // (N_Q_HEADS // N_KV_HEADS)) + h % 2
        return slice(v * V7X_LANES, (v + 1) * V7X_LANES)

    def scores(h):
        c = h // 2
        qc = qt_ref[0, c * V7X_LANES:(c + 1) * V7X_LANES, :]
        return jnp.dot(kx_ref[:, variant(h)], qc, preferred_element_type=F32)

    s_next = scores(0)
    halves = []
    for h in range(N_Q_HEADS):
        s = s_next
        if h + 1 < N_Q_HEADS:
            s_next = scores(h + 1)
        m = jnp.max(s, axis=0, keepdims=True)
        e = jnp.exp2(s - m).astype(BF16)
        r = jnp.dot(vxt_ref[0, variant(h), :], e, preferred_element_type=F32)
        halves.append(r / jnp.concatenate([r[HEAD_DIM:], r[:HEAD_DIM]], axis=0))
        if h % 2 == 1:
            c = h // 2
            pair = jnp.where(top, halves[0], halves[1])
            o_ref[:, c * V7X_LANES:(c + 1) * V7X_LANES] = pair.T.astype(BF16)
            halves = []


def _attention(qt, kx, vxt, batch, seq):
    nq = seq // Q_TILE
    return pl.pallas_call(
        _attn_kernel,
        grid=(batch, nq),
        in_specs=[
            pl.BlockSpec((1, ATTN_WIDTH, Q_TILE), lambda b, i: (b, 0, i)),
            pl.BlockSpec((seq, 4 * V7X_LANES), lambda b, i: (b, 0)),
            pl.BlockSpec((1, 4 * V7X_LANES, seq), lambda b, i: (b, 0, 0)),
        ],
        out_specs=pl.BlockSpec((Q_TILE, ATTN_WIDTH), lambda b, i: (b * nq + i, 0)),
        out_shape=jax.ShapeDtypeStruct((batch * seq, ATTN_WIDTH), BF16),
        compiler_params=pltpu.CompilerParams(
            dimension_semantics=("arbitrary", "arbitrary"), vmem_limit_bytes=VMEM_LIMIT),
        name="attention",
    )(qt, kx, vxt)


def _fourier_kernel(u_ref, dft_ref, cc_ref, sc_ref, wbd_ref, y_ref, ma_ref, mb_ref):
    @pl.when(pl.program_id(0) == 0)
    def _():
        w = wbd_ref[...]
        ma_ref[...] = jnp.dot(cc_ref[...], w, precision=lax.Precision.HIGHEST,
                              preferred_element_type=F32).astype(BF16)
        mb_ref[...] = jnp.dot(sc_ref[...], w, precision=lax.Precision.HIGHEST,
                              preferred_element_type=F32).astype(BF16)

    u = u_ref[...]
    a = jnp.dot(u, ma_ref[...], preferred_element_type=F32).astype(BF16)
    b = jnp.dot(u, mb_ref[...], preferred_element_type=F32).astype(BF16)
    ab = jnp.concatenate([a, b], axis=0)
    y_ref[...] = jnp.dot(dft_ref[...], ab, preferred_element_type=F32).astype(BF16)


def _fourier(u, dft, cc, sc, wbd, batch, seq):
    return pl.pallas_call(
        _fourier_kernel,
        grid=(batch,),
        in_specs=[
            pl.BlockSpec((seq, FOURIER_WIDTH), lambda b: (b, 0)),
            _resident((seq, 2 * seq)),
            _resident((FOURIER_WIDTH, FOURIER_WIDTH)),
            _resident((FOURIER_WIDTH, FOURIER_WIDTH)),
            _resident((FOURIER_WIDTH, FOURIER_WIDTH)),
        ],
        out_specs=pl.BlockSpec((seq, FOURIER_WIDTH), lambda b: (b, 0)),
        out_shape=jax.ShapeDtypeStruct((batch * seq, FOURIER_WIDTH), BF16),
        scratch_shapes=[pltpu.VMEM((FOURIER_WIDTH, FOURIER_WIDTH), BF16),
                        pltpu.VMEM((FOURIER_WIDTH, FOURIER_WIDTH), BF16)],
        compiler_params=pltpu.CompilerParams(
            dimension_semantics=("arbitrary",), vmem_limit_bytes=VMEM_LIMIT),
        name="fourier",
    )(u, dft, cc, sc, wbd)


def _out_mlp_kernel(a_ref, f_ref, x_ref, wo_ref, g2_ref, wu_ref, wd_ref, g3_ref, o_ref):
    mix = (jnp.dot(a_ref[...], wo_ref[:ATTN_WIDTH, :], preferred_element_type=F32)
           + jnp.dot(f_ref[...], wo_ref[ATTN_WIDTH:, :], preferred_element_type=F32))
    x1 = x_ref[...] + mix
    h = (x1 * _rms_scale(x1) * g2_ref[...]).astype(BF16)
    acc = x1
    for c in range(D_FF // FF_CHUNK):
        cs = slice(c * FF_CHUNK, (c + 1) * FF_CHUNK)
        z = jnp.dot(h, wu_ref[:, cs], preferred_element_type=F32)
        z = jnp.square(jnp.maximum(z, 0.0)).astype(BF16)
        acc = acc + jnp.dot(z, wd_ref[cs, :], preferred_element_type=F32)
    o_ref[...] = acc * _rms_scale(acc) * g3_ref[...]


def _out_mlp(attn, four, x2, w_out, g2, w_up, w_down, g3):
    n = x2.shape[0]
    tok = lambda i: (i, 0)
    return pl.pallas_call(
        _out_mlp_kernel,
        grid=(n // TOKEN_TILE,),
        in_specs=[
            pl.BlockSpec((TOKEN_TILE, ATTN_WIDTH), tok),
            pl.BlockSpec((TOKEN_TILE, FOURIER_WIDTH), tok),
            pl.BlockSpec((TOKEN_TILE, D_MODEL), tok),
            _resident((D_MODEL, D_MODEL)),
            _resident((1, D_MODEL)),
            _resident((D_MODEL, D_FF)),
            _resident((D_FF, D_MODEL)),
            _resident((1, D_MODEL)),
        ],
        out_specs=pl.BlockSpec((TOKEN_TILE, D_MODEL), tok),
        out_shape=jax.ShapeDtypeStruct((n, D_MODEL), F32),
        compiler_params=pltpu.CompilerParams(
            dimension_semantics=("arbitrary",), vmem_limit_bytes=VMEM_LIMIT),
        name="out_mlp",
    )(attn, four, x2, w_out, g2, w_up, w_down, g3)


def _rope_tables(seq):
    rows = seq // GRID_W
    row = jnp.repeat(jnp.arange(rows, dtype=jnp.int32), GRID_W)
    col = jnp.tile(jnp.arange(GRID_W, dtype=jnp.int32), rows)
    inv_freq = ROPE_THETA ** (-jnp.arange(0, AXIS_DIM, 2, dtype=F32) / AXIS_DIM)
    row_ang = row.astype(F32)[:, None] * inv_freq[None, :]
    col_ang = col.astype(F32)[:, None] * inv_freq[None, :]
    ang = jnp.concatenate([row_ang, row_ang, col_ang, col_ang], axis=-1)
    ang = jnp.concatenate([ang, ang], axis=-1)
    first_half = (jnp.arange(V7X_LANES) % AXIS_DIM) < (AXIS_DIM // 2)
    sin = jnp.sin(ang)
    return (jnp.cos(ang), jnp.where(first_half, -sin, 0.0), jnp.where(first_half, 0.0, sin))


@functools.lru_cache(maxsize=None)
def _dft_constants(seq):
    j = np.arange(seq, dtype=np.int64)
    ang = 2.0 * np.pi * ((j[:, None] * j[None, :]) % seq) / seq
    dft = np.concatenate([np.cos(ang), -np.sin(ang)], axis=1)
    c = np.arange(GROUP_DIM, dtype=np.int64)
    angc = 2.0 * np.pi * ((c[:, None] * c[None, :]) % GROUP_DIM) / GROUP_DIM
    scale = 1.0 / np.sqrt(float(seq * GROUP_DIM))
    eye = np.eye(N_GROUPS)
    cc = np.kron(eye, np.cos(angc)) * scale
    sc = np.kron(eye, np.sin(angc)) * scale
    ones_bd = np.kron(np.eye(V7X_LANES // HEAD_DIM), np.ones((HEAD_DIM, HEAD_DIM)))
    return (dft.astype(np.float32), cc.astype(np.float32), sc.astype(np.float32),
            ones_bd.astype(np.float32))


def kernel(x, mix_norm_g, w_in, q_norm_g, k_norm_g, w_fourier, w_out, mlp_norm_g, w_up,
           w_down, final_norm_g):
    batch, seq, d_model = x.shape
    assert d_model == D_MODEL and seq % TOKEN_TILE == 0 and seq % Q_TILE == 0
    x2 = x.reshape(batch * seq, d_model)

    dft_np, cc_np, sc_np, ones_np = _dft_constants(seq)
    dft = jnp.asarray(dft_np).astype(BF16)
    cc = jnp.asarray(cc_np)
    sc = jnp.asarray(sc_np)
    ones_bd = jnp.asarray(ones_np, dtype=BF16)
    cos_t, sin_a, sin_b = _rope_tables(seq)

    row = lambda g: g.reshape(1, -1).astype(F32)
    two_heads = lambda g: jnp.tile(g.astype(F32), V7X_LANES // HEAD_DIM).reshape(1, V7X_LANES)
    wbd = (jnp.eye(N_GROUPS, dtype=F32)[:, None, :, None]
           * w_fourier.astype(F32)[:, :, None, :]).reshape(FOURIER_WIDTH, FOURIER_WIDTH)

    qt, kx, vxt, u = _in_proj(x2, row(mix_norm_g), w_in.astype(BF16), ones_bd,
                            two_heads(q_norm_g), two_heads(k_norm_g),
                            cos_t, sin_a, sin_b, batch, seq)
    attn = _attention(qt, kx, vxt, batch, seq)
    four = _fourier(u, dft, cc, sc, wbd, batch, seq)
    out = _out_mlp(attn, four, x2, w_out.astype(BF16), row(mlp_norm_g),
                   w_up.astype(BF16), w_down.astype(BF16), row(final_norm_g))
    return out.reshape(batch, seq, d_model)
```

```python
import functools

import numpy as np
import jax
import jax.numpy as jnp
from jax import lax
from jax.experimental import pallas as pl
from jax.experimental.pallas import tpu as pltpu

D_MODEL = 1024
HEAD_DIM = 64
N_Q_HEADS = 8
N_KV_HEADS = 2
ATTN_WIDTH = N_Q_HEADS * HEAD_DIM
KV_WIDTH = N_KV_HEADS * HEAD_DIM
N_GROUPS = 8
GROUP_DIM = 64
FOURIER_WIDTH = N_GROUPS * GROUP_DIM
IN_PROJ_WIDTH = ATTN_WIDTH + 2 * KV_WIDTH + FOURIER_WIDTH
D_FF = 4 * D_MODEL
GRID_W = 64
AXIS_DIM = HEAD_DIM // 2
ROPE_THETA = 10000.0
NORM_EPS = 1e-6
LOG2_E = 1.4426950408889634

V7X_LANES = 128
V7X_VMEM_BYTES = 64 * 1024 * 1024

TOKEN_TILE = 512
IN_TILE = 2048
IN_SUB = 256
Q_TILE = 512
FF_CHUNK = 1024
VMEM_LIMIT = 56 * 1024 * 1024

BF16 = jnp.bfloat16
F32 = jnp.float32


def _resident(shape):
    zeros = (0,) * len(shape)
    return pl.BlockSpec(shape, lambda *_: zeros, pipeline_mode=pl.Buffered(1))


def _rms_scale(x):
    return lax.rsqrt(jnp.mean(x * x, axis=-1, keepdims=True) + NORM_EPS)


def _in_proj_kernel(x_ref, g_ref, w_ref, ones_ref, qg_ref, kg_ref, cos_ref, sa_ref,
                    sb_ref, qt_ref, kx_ref, vxt_ref, u_ref):
    g = g_ref[...]
    ones = ones_ref[...]
    qg = qg_ref[...]
    kg = kg_ref[...]

    def project(i):
        x = x_ref[i * IN_SUB:(i + 1) * IN_SUB, :]
        h = x * _rms_scale(x) * g
        return jnp.dot(h.astype(BF16), w_ref[...], preferred_element_type=F32)

    def finish(i, proj):
        rows = slice(i * IN_SUB, (i + 1) * IN_SUB)
        cos = cos_ref[rows, :]
        sin_a = sa_ref[rows, :]
        sin_b = sb_ref[rows, :]

        def norm_rope(blk, gain, scale):
            ss = jnp.dot((blk * blk).astype(BF16), ones, preferred_element_type=F32)
            y = blk * (lax.rsqrt(ss * (1.0 / HEAD_DIM) + NORM_EPS) * scale) * gain
            return (y * cos + pltpu.roll(y, V7X_LANES - 16, 1) * sin_a
                    + pltpu.roll(y, 16, 1) * sin_b)

        for c in range(ATTN_WIDTH // V7X_LANES):
            sl = slice(c * V7X_LANES, (c + 1) * V7X_LANES)
            qt_ref[0, sl, rows] = norm_rope(
                proj[:, sl], qg, HEAD_DIM ** -0.5 * LOG2_E).T.astype(BF16)

        k = norm_rope(proj[:, ATTN_WIDTH:ATTN_WIDTH + KV_WIDTH], kg, 1.0)
        kr = pltpu.roll(k, HEAD_DIM, 1)
        lo = lax.broadcasted_iota(jnp.int32, k.shape, 1) < HEAD_DIM
        for v, blk in enumerate((jnp.where(lo, k, 0.0), jnp.where(lo, 0.0, kr),
                                 jnp.where(lo, kr, 0.0), jnp.where(lo, 0.0, k))):
            kx_ref[rows, v * V7X_LANES:(v + 1) * V7X_LANES] = blk.astype(BF16)

        vt = proj[:, ATTN_WIDTH + KV_WIDTH:ATTN_WIDTH + 2 * KV_WIDTH].T.astype(BF16)
        top, bot = vt[:HEAD_DIM], vt[HEAD_DIM:]
        one = jnp.ones_like(top)
        for v, (first, second) in enumerate(((top, one), (one, top), (bot, one), (one, bot))):
            vxt_ref[0, v * V7X_LANES:v * V7X_LANES + HEAD_DIM, rows] = first
            vxt_ref[0, v * V7X_LANES + HEAD_DIM:(v + 1) * V7X_LANES, rows] = second

        u_ref[rows, :] = proj[:, ATTN_WIDTH + 2 * KV_WIDTH:].astype(BF16)

    n_sub = IN_TILE // IN_SUB
    nxt = project(0)
    for i in range(n_sub):
        cur = nxt
        if i + 1 < n_sub:
            nxt = project(i + 1)
        finish(i, cur)


def _in_proj(x2, mix_g, w_in, ones_bd, qg, kg, cos_t, sin_a, sin_b, batch, seq):
    n = x2.shape[0]
    tiles_per_seq = seq // IN_TILE
    tok = lambda i: (i, 0)
    pos = lambda i: (i % tiles_per_seq, 0)
    chan_major = lambda i: (i // tiles_per_seq, 0, i % tiles_per_seq)
    return pl.pallas_call(
        _in_proj_kernel,
        grid=(n // IN_TILE,),
        in_specs=[
            pl.BlockSpec((IN_TILE, D_MODEL), tok),
            _resident((1, D_MODEL)),
            _resident((D_MODEL, IN_PROJ_WIDTH)),
            _resident((V7X_LANES, V7X_LANES)),
            _resident((1, V7X_LANES)),
            _resident((1, V7X_LANES)),
            pl.BlockSpec((IN_TILE, V7X_LANES), pos),
            pl.BlockSpec((IN_TILE, V7X_LANES), pos),
            pl.BlockSpec((IN_TILE, V7X_LANES), pos),
        ],
        out_specs=[
            pl.BlockSpec((1, ATTN_WIDTH, IN_TILE), chan_major),
            pl.BlockSpec((IN_TILE, 4 * V7X_LANES), tok),
            pl.BlockSpec((1, 4 * V7X_LANES, IN_TILE), chan_major),
            pl.BlockSpec((IN_TILE, FOURIER_WIDTH), tok),
        ],
        out_shape=[
            jax.ShapeDtypeStruct((batch, ATTN_WIDTH, seq), BF16),
            jax.ShapeDtypeStruct((n, 4 * V7X_LANES), BF16),
            jax.ShapeDtypeStruct((batch, 4 * V7X_LANES, seq), BF16),
            jax.ShapeDtypeStruct((n, FOURIER_WIDTH), BF16),
        ],
        compiler_params=pltpu.CompilerParams(
            dimension_semantics=("arbitrary",), vmem_limit_bytes=VMEM_LIMIT),
        name="in_proj",
    )(x2, mix_g, w_in, ones_bd, qg, kg, cos_t, sin_a, sin_b)


def _attn_kernel(qt_ref, kx_ref, vxt_ref, o_ref):
    top = lax.broadcasted_iota(jnp.int32, (V7X_LANES, Q_TILE), 0) < HEAD_DIM

    def variant(h):
        v = 2 * (h // (N_Q_HEADS // N_KV_HEADS)) + h % 2
        return slice(v * V7X_LANES, (v + 1) * V7X_LANES)

    def scores(h):
        c = h // 2
        qc = qt_ref[0, c * V7X_LANES:(c + 1) * V7X_LANES, :]
        return jnp.dot(kx_ref[:, variant(h)], qc, preferred_element_type=F32)

    s_next = scores(0)
    halves = []
    for h in range(N_Q_HEADS):
        s = s_next
        if h + 1 < N_Q_HEADS:
            s_next = scores(h + 1)
        m = jnp.max(s, axis=0, keepdims=True)
        e = jnp.exp2(s - m).astype(BF16)
        r = jnp.dot(vxt_ref[0, variant(h), :], e, preferred_element_type=F32)
        halves.append(r / jnp.concatenate([r[HEAD_DIM:], r[:HEAD_DIM]], axis=0))
        if h % 2 == 1:
            c = h // 2
            pair = jnp.where(top, halves[0], halves[1])
            o_ref[:, c * V7X_LANES:(c + 1) * V7X_LANES] = pair.T.astype(BF16)
            halves = []


def _attention(qt, kx, vxt, batch, seq):
    nq = seq // Q_TILE
    return pl.pallas_call(
        _attn_kernel,
        grid=(batch, nq),
        in_specs=[
            pl.BlockSpec((1, ATTN_WIDTH, Q_TILE), lambda b, i: (b, 0, i)),
            pl.BlockSpec((seq, 4 * V7X_LANES), lambda b, i: (b, 0)),
            pl.BlockSpec((1, 4 * V7X_LANES, seq), lambda b, i: (b, 0, 0)),
        ],
        out_specs=pl.BlockSpec((Q_TILE, ATTN_WIDTH), lambda b, i: (b * nq + i, 0)),
        out_shape=jax.ShapeDtypeStruct((batch * seq, ATTN_WIDTH), BF16),
        compiler_params=pltpu.CompilerParams(
            dimension_semantics=("arbitrary", "arbitrary"), vmem_limit_bytes=VMEM_LIMIT),
        name="attention",
    )(qt, kx, vxt)


def _fourier_kernel(u_ref, dft_ref, cc_ref, sc_ref, wbd_ref, y_ref, ma_ref, mb_ref):
    @pl.when(pl.program_id(0) == 0)
    def _():
        w = wbd_ref[...]
        ma_ref[...] = jnp.dot(cc_ref[...], w, precision=lax.Precision.HIGHEST,
                              preferred_element_type=F32).astype(BF16)
        mb_ref[...] = jnp.dot(sc_ref[...], w, precision=lax.Precision.HIGHEST,
                              preferred_element_type=F32).astype(BF16)

    u = u_ref[...]
    a = jnp.dot(u, ma_ref[...], preferred_element_type=F32).astype(BF16)
    b = jnp.dot(u, mb_ref[...], preferred_element_type=F32).astype(BF16)
    ab = jnp.concatenate([a, b], axis=0)
    y_ref[...] = jnp.dot(dft_ref[...], ab, preferred_element_type=F32).astype(BF16)


def _fourier(u, dft, cc, sc, wbd, batch, seq):
    return pl.pallas_call(
        _fourier_kernel,
        grid=(batch,),
        in_specs=[
            pl.BlockSpec((seq, FOURIER_WIDTH), lambda b: (b, 0)),
            _resident((seq, 2 * seq)),
            _resident((FOURIER_WIDTH, FOURIER_WIDTH)),
            _resident((FOURIER_WIDTH, FOURIER_WIDTH)),
            _resident((FOURIER_WIDTH, FOURIER_WIDTH)),
        ],
        out_specs=pl.BlockSpec((seq, FOURIER_WIDTH), lambda b: (b, 0)),
        out_shape=jax.ShapeDtypeStruct((batch * seq, FOURIER_WIDTH), BF16),
        scratch_shapes=[pltpu.VMEM((FOURIER_WIDTH, FOURIER_WIDTH), BF16),
                        pltpu.VMEM((FOURIER_WIDTH, FOURIER_WIDTH), BF16)],
        compiler_params=pltpu.CompilerParams(
            dimension_semantics=("arbitrary",), vmem_limit_bytes=VMEM_LIMIT),
        name="fourier",
    )(u, dft, cc, sc, wbd)


def _out_mlp_kernel(a_ref, f_ref, x_ref, wo_ref, g2_ref, wu_ref, wd_ref, g3_ref, o_ref):
    mix = (jnp.dot(a_ref[...], wo_ref[:ATTN_WIDTH, :], preferred_element_type=F32)
           + jnp.dot(f_ref[...], wo_ref[ATTN_WIDTH:, :], preferred_element_type=F32))
    x1 = x_ref[...] + mix
    h = (x1 * _rms_scale(x1) * g2_ref[...]).astype(BF16)
    acc = x1
    for c in range(D_FF // FF_CHUNK):
        cs = slice(c * FF_CHUNK, (c + 1) * FF_CHUNK)
        z = jnp.dot(h, wu_ref[:, cs], preferred_element_type=F32)
        z = jnp.square(jnp.maximum(z, 0.0)).astype(BF16)
        acc = acc + jnp.dot(z, wd_ref[cs, :], preferred_element_type=F32)
    o_ref[...] = acc * _rms_scale(acc) * g3_ref[...]


def _out_mlp(attn, four, x2, w_out, g2, w_up, w_down, g3):
    n = x2.shape[0]
    tok = lambda i: (i, 0)
    return pl.pallas_call(
        _out_mlp_kernel,
        grid=(n // TOKEN_TILE,),
        in_specs=[
            pl.BlockSpec((TOKEN_TILE, ATTN_WIDTH), tok),
            pl.BlockSpec((TOKEN_TILE, FOURIER_WIDTH), tok),
            pl.BlockSpec((TOKEN_TILE, D_MODEL), tok),
            _resident((D_MODEL, D_MODEL)),
            _resident((1, D_MODEL)),
            _resident((D_MODEL, D_FF)),
            _resident((D_FF, D_MODEL)),
            _resident((1, D_MODEL)),
        ],
        out_specs=pl.BlockSpec((TOKEN_TILE, D_MODEL), tok),
        out_shape=jax.ShapeDtypeStruct((n, D_MODEL), F32),
        compiler_params=pltpu.CompilerParams(
            dimension_semantics=("arbitrary",), vmem_limit_bytes=VMEM_LIMIT),
        name="out_mlp",
    )(attn, four, x2, w_out, g2, w_up, w_down, g3)


def _rope_tables(seq):
    rows = seq // GRID_W
    row = jnp.repeat(jnp.arange(rows, dtype=jnp.int32), GRID_W)
    col = jnp.tile(jnp.arange(GRID_W, dtype=jnp.int32), rows)
    inv_freq = ROPE_THETA ** (-jnp.arange(0, AXIS_DIM, 2, dtype=F32) / AXIS_DIM)
    row_ang = row.astype(F32)[:, None] * inv_freq[None, :]
    col_ang = col.astype(F32)[:, None] * inv_freq[None, :]
    ang = jnp.concatenate([row_ang, row_ang, col_ang, col_ang], axis=-1)
    ang = jnp.concatenate([ang, ang], axis=-1)
    first_half = (jnp.arange(V7X_LANES) % AXIS_DIM) < (AXIS_DIM // 2)
    sin = jnp.sin(ang)
    return (jnp.cos(ang), jnp.where(first_half, -sin, 0.0), jnp.where(first_half, 0.0, sin))


@functools.lru_cache(maxsize=None)
def _dft_constants(seq):
    j = np.arange(seq, dtype=np.int64)
    ang = 2.0 * np.pi * ((j[:, None] * j[None, :]) % seq) / seq
    dft = np.concatenate([np.cos(ang), -np.sin(ang)], axis=1)
    c = np.arange(GROUP_DIM, dtype=np.int64)
    angc = 2.0 * np.pi * ((c[:, None] * c[None, :]) % GROUP_DIM) / GROUP_DIM
    scale = 1.0 / np.sqrt(float(seq * GROUP_DIM))
    eye = np.eye(N_GROUPS)
    cc = np.kron(eye, np.cos(angc)) * scale
    sc = np.kron(eye, np.sin(angc)) * scale
    ones_bd = np.kron(np.eye(V7X_LANES // HEAD_DIM), np.ones((HEAD_DIM, HEAD_DIM)))
    return (dft.astype(np.float32), cc.astype(np.float32), sc.astype(np.float32),
            ones_bd.astype(np.float32))


def kernel(x, mix_norm_g, w_in, q_norm_g, k_norm_g, w_fourier, w_out, mlp_norm_g, w_up,
           w_down, final_norm_g):
    batch, seq, d_model = x.shape
    assert d_model == D_MODEL and seq % TOKEN_TILE == 0 and seq % Q_TILE == 0 and seq % IN_TILE == 0
    x2 = x.reshape(batch * seq, d_model)

    dft_np, cc_np, sc_np, ones_np = _dft_constants(seq)
    dft = jnp.asarray(dft_np).astype(BF16)
    cc = jnp.asarray(cc_np)
    sc = jnp.asarray(sc_np)
    ones_bd = jnp.asarray(ones_np, dtype=BF16)
    cos_t, sin_a, sin_b = _rope_tables(seq)

    row = lambda g: g.reshape(1, -1).astype(F32)
    two_heads = lambda g: jnp.tile(g.astype(F32), V7X_LANES // HEAD_DIM).reshape(1, V7X_LANES)
    wbd = (jnp.eye(N_GROUPS, dtype=F32)[:, None, :, None]
           * w_fourier.astype(F32)[:, :, None, :]).reshape(FOURIER_WIDTH, FOURIER_WIDTH)

    qt, kx, vxt, u = _in_proj(x2, row(mix_norm_g), w_in.astype(BF16), ones_bd,
                            two_heads(q_norm_g), two_heads(k_norm_g),
                            cos_t, sin_a, sin_b, batch, seq)
    attn = _attention(qt, kx, vxt, batch, seq)
    four = _fourier(u, dft, cc, sc, wbd, batch, seq)
    out = _out_mlp(attn, four, x2, w_out.astype(BF16), row(mlp_norm_g),
                   w_up.astype(BF16), w_down.astype(BF16), row(final_norm_g))
    return out.reshape(batch, seq, d_model)
```

```python
import functools

import numpy as np
import jax
import jax.numpy as jnp
from jax import lax
from jax.experimental import pallas as pl
from jax.experimental.pallas import tpu as pltpu

D_MODEL = 1024
HEAD_DIM = 64
N_Q_HEADS = 8
N_KV_HEADS = 2
ATTN_WIDTH = N_Q_HEADS * HEAD_DIM
KV_WIDTH = N_KV_HEADS * HEAD_DIM
N_GROUPS = 8
GROUP_DIM = 64
FOURIER_WIDTH = N_GROUPS * GROUP_DIM
IN_PROJ_WIDTH = ATTN_WIDTH + 2 * KV_WIDTH + FOURIER_WIDTH
D_FF = 4 * D_MODEL
GRID_W = 64
AXIS_DIM = HEAD_DIM // 2
ROPE_THETA = 10000.0
NORM_EPS = 1e-6
LOG2_E = 1.4426950408889634

V7X_LANES = 128
V7X_VMEM_BYTES = 64 * 1024 * 1024

TOKEN_TILE = 512
IN_TILE = 2048
IN_SUB = 256
Q_TILE = 512
FF_CHUNK = 1024
DFT_N1 = 32
DFT_N2 = 64
DFT_LANE_CHUNK = 4096
DFT_K1_GROUP = 4
VMEM_LIMIT = 56 * 1024 * 1024

BF16 = jnp.bfloat16
F32 = jnp.float32


def _resident(shape):
    zeros = (0,) * len(shape)
    return pl.BlockSpec(shape, lambda *_: zeros, pipeline_mode=pl.Buffered(1))


def _rms_scale(x):
    return lax.rsqrt(jnp.mean(x * x, axis=-1, keepdims=True) + NORM_EPS)


def _in_proj_kernel(x_ref, g_ref, w_ref, ones_ref, qg_ref, kg_ref, cos_ref, sa_ref,
                    sb_ref, qt_ref, kx_ref, vxt_ref, u_ref):
    g = g_ref[...]
    ones = ones_ref[...]
    qg = qg_ref[...]
    kg = kg_ref[...]

    def project(i):
        x = x_ref[i * IN_SUB:(i + 1) * IN_SUB, :]
        h = x * _rms_scale(x) * g
        return jnp.dot(h.astype(BF16), w_ref[...], preferred_element_type=F32)

    def finish(i, proj):
        rows = slice(i * IN_SUB, (i + 1) * IN_SUB)
        cos = cos_ref[rows, :]
        sin_a = sa_ref[rows, :]
        sin_b = sb_ref[rows, :]

        def norm_rope(blk, gain, scale):
            ss = jnp.dot((blk * blk).astype(BF16), ones, preferred_element_type=F32)
            y = blk * (lax.rsqrt(ss * (1.0 / HEAD_DIM) + NORM_EPS) * scale) * gain
            return (y * cos + pltpu.roll(y, V7X_LANES - 16, 1) * sin_a
                    + pltpu.roll(y, 16, 1) * sin_b)

        for c in range(ATTN_WIDTH // V7X_LANES):
            sl = slice(c * V7X_LANES, (c + 1) * V7X_LANES)
            qt_ref[0, sl, rows] = norm_rope(
                proj[:, sl], qg, HEAD_DIM ** -0.5 * LOG2_E).T.astype(BF16)

        k = norm_rope(proj[:, ATTN_WIDTH:ATTN_WIDTH + KV_WIDTH], kg, 1.0)
        kr = pltpu.roll(k, HEAD_DIM, 1)
        lo = lax.broadcasted_iota(jnp.int32, k.shape, 1) < HEAD_DIM
        for v, blk in enumerate((jnp.where(lo, k, 0.0), jnp.where(lo, 0.0, kr),
                                 jnp.where(lo, kr, 0.0), jnp.where(lo, 0.0, k))):
            kx_ref[rows, v * V7X_LANES:(v + 1) * V7X_LANES] = blk.astype(BF16)

        vt = proj[:, ATTN_WIDTH + KV_WIDTH:ATTN_WIDTH + 2 * KV_WIDTH].T.astype(BF16)
        top, bot = vt[:HEAD_DIM], vt[HEAD_DIM:]
        one = jnp.ones_like(top)
        for v, (first, second) in enumerate(((top, one), (one, top), (bot, one), (one, bot))):
            vxt_ref[0, v * V7X_LANES:v * V7X_LANES + HEAD_DIM, rows] = first
            vxt_ref[0, v * V7X_LANES + HEAD_DIM:(v + 1) * V7X_LANES, rows] = second

        u_ref[rows, :] = proj[:, ATTN_WIDTH + 2 * KV_WIDTH:].astype(BF16)

    n_sub = IN_TILE // IN_SUB
    nxt = project(0)
    for i in range(n_sub):
        cur = nxt
        if i + 1 < n_sub:
            nxt = project(i + 1)
        finish(i, cur)


def _in_proj(x2, mix_g, w_in, ones_bd, qg, kg, cos_t, sin_a, sin_b, batch, seq):
    n = x2.shape[0]
    tiles_per_seq = seq // IN_TILE
    tok = lambda i: (i, 0)
    pos = lambda i: (i % tiles_per_seq, 0)
    chan_major = lambda i: (i // tiles_per_seq, 0, i % tiles_per_seq)
    return pl.pallas_call(
        _in_proj_kernel,
        grid=(n // IN_TILE,),
        in_specs=[
            pl.BlockSpec((IN_TILE, D_MODEL), tok),
            _resident((1, D_MODEL)),
            _resident((D_MODEL, IN_PROJ_WIDTH)),
            _resident((V7X_LANES, V7X_LANES)),
            _resident((1, V7X_LANES)),
            _resident((1, V7X_LANES)),
            pl.BlockSpec((IN_TILE, V7X_LANES), pos),
            pl.BlockSpec((IN_TILE, V7X_LANES), pos),
            pl.BlockSpec((IN_TILE, V7X_LANES), pos),
        ],
        out_specs=[
            pl.BlockSpec((1, ATTN_WIDTH, IN_TILE), chan_major),
            pl.BlockSpec((IN_TILE, 4 * V7X_LANES), tok),
            pl.BlockSpec((1, 4 * V7X_LANES, IN_TILE), chan_major),
            pl.BlockSpec((IN_TILE, FOURIER_WIDTH), tok),
        ],
        out_shape=[
            jax.ShapeDtypeStruct((batch, ATTN_WIDTH, seq), BF16),
            jax.ShapeDtypeStruct((n, 4 * V7X_LANES), BF16),
            jax.ShapeDtypeStruct((batch, 4 * V7X_LANES, seq), BF16),
            jax.ShapeDtypeStruct((n, FOURIER_WIDTH), BF16),
        ],
        compiler_params=pltpu.CompilerParams(
            dimension_semantics=("arbitrary",), vmem_limit_bytes=VMEM_LIMIT),
        name="in_proj",
    )(x2, mix_g, w_in, ones_bd, qg, kg, cos_t, sin_a, sin_b)


def _attn_kernel(qt_ref, kx_ref, vxt_ref, o_ref):
    top = lax.broadcasted_iota(jnp.int32, (V7X_LANES, Q_TILE), 0) < HEAD_DIM

    def variant(h):
        v = 2 * (h // (N_Q_HEADS // N_KV_HEADS)) + h % 2
        return slice(v * V7X_LANES, (v + 1) * V7X_LANES)

    def scores(h):
        c = h // 2
        qc = qt_ref[0, c * V7X_LANES:(c + 1) * V7X_LANES, :]
        return jnp.dot(kx_ref[:, variant(h)], qc, preferred_element_type=F32)

    s_next = scores(0)
    halves = []
    for h in range(N_Q_HEADS):
        s = s_next
        if h + 1 < N_Q_HEADS:
            s_next = scores(h + 1)
        m = jnp.max(s, axis=0, keepdims=True)
        e = jnp.exp2(s - m).astype(BF16)
        r = jnp.dot(vxt_ref[0, variant(h), :], e, preferred_element_type=F32)
        halves.append(r / jnp.concatenate([r[HEAD_DIM:], r[:HEAD_DIM]], axis=0))
        if h % 2 == 1:
            c = h // 2
            pair = jnp.where(top, halves[0], halves[1])
            o_ref[:, c * V7X_LANES:(c + 1) * V7X_LANES] = pair.T.astype(BF16)
            halves = []


def _attention(qt, kx, vxt, batch, seq):
    nq = seq // Q_TILE
    return pl.pallas_call(
        _attn_kernel,
        grid=(batch, nq),
        in_specs=[
            pl.BlockSpec((1, ATTN_WIDTH, Q_TILE), lambda b, i: (b, 0, i)),
            pl.BlockSpec((seq, 4 * V7X_LANES), lambda b, i: (b, 0)),
            pl.BlockSpec((1, 4 * V7X_LANES, seq), lambda b, i: (b, 0, 0)),
        ],
        out_specs=pl.BlockSpec((Q_TILE, ATTN_WIDTH), lambda b, i: (b * nq + i, 0)),
        out_shape=jax.ShapeDtypeStruct((batch * seq, ATTN_WIDTH), BF16),
        compiler_params=pltpu.CompilerParams(
            dimension_semantics=("arbitrary", "arbitrary"), vmem_limit_bytes=VMEM_LIMIT),
        name="attention",
    )(qt, kx, vxt)


def _dft_stage1_kernel(u_ref, m1_ref, tr_ref, ts_ref, q_ref):
    m1 = m1_ref[...]
    for j in range(u_ref.shape[2] // DFT_LANE_CHUNK):
        cols = slice(j * DFT_LANE_CHUNK, (j + 1) * DFT_LANE_CHUNK)
        p = jnp.dot(m1, u_ref[0, :, cols], preferred_element_type=F32)
        pr, pi = p[:DFT_N1], p[DFT_N1:]
        tr, ts = tr_ref[:, cols], ts_ref[:, cols]
        q_ref[0, :DFT_N1, cols] = (pr * tr + pi * ts).astype(BF16)
        q_ref[0, DFT_N1:, cols] = (pi * tr - pr * ts).astype(BF16)


def _dft_stage1(u3, m1, tr, ts):
    batch, n1, width = u3.shape
    return pl.pallas_call(
        _dft_stage1_kernel,
        grid=(batch,),
        in_specs=[
            pl.BlockSpec((1, n1, width), lambda b: (b, 0, 0)),
            _resident((2 * n1, n1)),
            _resident((n1, width)),
            _resident((n1, width)),
        ],
        out_specs=pl.BlockSpec((1, 2 * n1, width), lambda b: (b, 0, 0)),
        out_shape=jax.ShapeDtypeStruct((batch, 2 * n1, width), BF16),
        compiler_params=pltpu.CompilerParams(
            dimension_semantics=("arbitrary",), vmem_limit_bytes=VMEM_LIMIT),
        name="dft_stage1",
    )(u3, m1, tr, ts)


def _dft_stage2_kernel(q_ref, r2_ref, cc_ref, sc_ref, wbd_ref, y_ref, ma_ref, mb_ref):
    @pl.when(pl.program_id(0) == 0)
    def _():
        w = wbd_ref[...]
        ma_ref[...] = jnp.dot(cc_ref[...], w, precision=lax.Precision.HIGHEST,
                              preferred_element_type=F32).astype(BF16)
        mb_ref[...] = jnp.dot(sc_ref[...], w, precision=lax.Precision.HIGHEST,
                              preferred_element_type=F32).astype(BF16)

    r2 = r2_ref[...]
    half = FOURIER_WIDTH // 2
    for g in range(DFT_N1 // DFT_K1_GROUP):
        xr, xi = [], []
        for k1 in range(g * DFT_K1_GROUP, (g + 1) * DFT_K1_GROUP):
            qk = jnp.concatenate([q_ref[0, 0, k1], q_ref[0, 1, k1]], axis=0)
            xk = jnp.dot(r2, qk, preferred_element_type=F32)
            xr.append(xk[:DFT_N2].astype(BF16))
            xi.append(xk[DFT_N2:].astype(BF16))
        xr = jnp.concatenate(xr, axis=0)
        xi = jnp.concatenate(xi, axis=0)
        for lanes in (slice(0, half), slice(half, FOURIER_WIDTH)):
            y = (jnp.dot(xr[:, lanes], ma_ref[lanes, lanes], preferred_element_type=F32)
                 + jnp.dot(xi[:, lanes], mb_ref[lanes, lanes], preferred_element_type=F32))
            for t in range(DFT_K1_GROUP):
                k1 = g * DFT_K1_GROUP + t
                y_ref[0, :, k1 * FOURIER_WIDTH + lanes.start:k1 * FOURIER_WIDTH + lanes.stop] = (
                    y[t * DFT_N2:(t + 1) * DFT_N2].astype(BF16))


def _dft_stage2(q5, r2, cc, sc, wbd):
    batch = q5.shape[0]
    return pl.pallas_call(
        _dft_stage2_kernel,
        grid=(batch,),
        in_specs=[
            pl.BlockSpec((1, 2, DFT_N1, DFT_N2, FOURIER_WIDTH), lambda b: (b, 0, 0, 0, 0)),
            _resident((2 * DFT_N2, 2 * DFT_N2)),
            _resident((FOURIER_WIDTH, FOURIER_WIDTH)),
            _resident((FOURIER_WIDTH, FOURIER_WIDTH)),
            _resident((FOURIER_WIDTH, FOURIER_WIDTH)),
        ],
        out_specs=pl.BlockSpec((1, DFT_N2, DFT_N1 * FOURIER_WIDTH), lambda b: (b, 0, 0)),
        out_shape=jax.ShapeDtypeStruct((batch, DFT_N2, DFT_N1 * FOURIER_WIDTH), BF16),
        scratch_shapes=[pltpu.VMEM((FOURIER_WIDTH, FOURIER_WIDTH), BF16),
                        pltpu.VMEM((FOURIER_WIDTH, FOURIER_WIDTH), BF16)],
        compiler_params=pltpu.CompilerParams(
            dimension_semantics=("arbitrary",), vmem_limit_bytes=VMEM_LIMIT),
        name="dft_stage2",
    )(q5, r2, cc, sc, wbd)


def _out_mlp_kernel(a_ref, f_ref, x_ref, wo_ref, g2_ref, wu_ref, wd_ref, g3_ref, o_ref):
    mix = (jnp.dot(a_ref[...], wo_ref[:ATTN_WIDTH, :], preferred_element_type=F32)
           + jnp.dot(f_ref[...], wo_ref[ATTN_WIDTH:, :], preferred_element_type=F32))
    x1 = x_ref[...] + mix
    h = (x1 * _rms_scale(x1) * g2_ref[...]).astype(BF16)
    acc = x1
    for c in range(D_FF // FF_CHUNK):
        cs = slice(c * FF_CHUNK, (c + 1) * FF_CHUNK)
        z = jnp.dot(h, wu_ref[:, cs], preferred_element_type=F32)
        z = jnp.square(jnp.maximum(z, 0.0)).astype(BF16)
        acc = acc + jnp.dot(z, wd_ref[cs, :], preferred_element_type=F32)
    o_ref[...] = acc * _rms_scale(acc) * g3_ref[...]


def _out_mlp(attn, four, x2, w_out, g2, w_up, w_down, g3):
    n = x2.shape[0]
    tok = lambda i: (i, 0)
    return pl.pallas_call(
        _out_mlp_kernel,
        grid=(n // TOKEN_TILE,),
        in_specs=[
            pl.BlockSpec((TOKEN_TILE, ATTN_WIDTH), tok),
            pl.BlockSpec((TOKEN_TILE, FOURIER_WIDTH), tok),
            pl.BlockSpec((TOKEN_TILE, D_MODEL), tok),
            _resident((D_MODEL, D_MODEL)),
            _resident((1, D_MODEL)),
            _resident((D_MODEL, D_FF)),
            _resident((D_FF, D_MODEL)),
            _resident((1, D_MODEL)),
        ],
        out_specs=pl.BlockSpec((TOKEN_TILE, D_MODEL), tok),
        out_shape=jax.ShapeDtypeStruct((n, D_MODEL), F32),
        compiler_params=pltpu.CompilerParams(
            dimension_semantics=("arbitrary",), vmem_limit_bytes=VMEM_LIMIT),
        name="out_mlp",
    )(attn, four, x2, w_out, g2, w_up, w_down, g3)


def _rope_tables(seq):
    rows = seq // GRID_W
    row = jnp.repeat(jnp.arange(rows, dtype=jnp.int32), GRID_W)
    col = jnp.tile(jnp.arange(GRID_W, dtype=jnp.int32), rows)
    inv_freq = ROPE_THETA ** (-jnp.arange(0, AXIS_DIM, 2, dtype=F32) / AXIS_DIM)
    row_ang = row.astype(F32)[:, None] * inv_freq[None, :]
    col_ang = col.astype(F32)[:, None] * inv_freq[None, :]
    ang = jnp.concatenate([row_ang, row_ang, col_ang, col_ang], axis=-1)
    ang = jnp.concatenate([ang, ang], axis=-1)
    first_half = (jnp.arange(V7X_LANES) % AXIS_DIM) < (AXIS_DIM // 2)
    sin = jnp.sin(ang)
    return (jnp.cos(ang), jnp.where(first_half, -sin, 0.0), jnp.where(first_half, 0.0, sin))


@functools.lru_cache(maxsize=None)
def _dft_constants(seq):
    assert seq == DFT_N1 * DFT_N2
    k1 = np.arange(DFT_N1, dtype=np.int64)
    a1 = 2.0 * np.pi * ((k1[:, None] * k1[None, :]) % DFT_N1) / DFT_N1
    m1 = np.concatenate([np.cos(a1), -np.sin(a1)], axis=0)
    n2 = np.arange(DFT_N2, dtype=np.int64)
    at = 2.0 * np.pi * (k1[:, None] * n2[None, :]) / seq
    tr = np.repeat(np.cos(at), FOURIER_WIDTH, axis=1)
    ts = np.repeat(np.sin(at), FOURIER_WIDTH, axis=1)
    a2 = 2.0 * np.pi * ((n2[:, None] * n2[None, :]) % DFT_N2) / DFT_N2
    c2, s2 = np.cos(a2), np.sin(a2)
    r2 = np.block([[c2, s2], [-s2, c2]])
    c = np.arange(GROUP_DIM, dtype=np.int64)
    angc = 2.0 * np.pi * ((c[:, None] * c[None, :]) % GROUP_DIM) / GROUP_DIM
    scale = 1.0 / np.sqrt(float(seq * GROUP_DIM))
    eye = np.eye(N_GROUPS)
    cc = np.kron(eye, np.cos(angc)) * scale
    sc = np.kron(eye, np.sin(angc)) * scale
    ones_bd = np.kron(np.eye(V7X_LANES // HEAD_DIM), np.ones((HEAD_DIM, HEAD_DIM)))
    f32 = lambda a: a.astype(np.float32)
    return f32(m1), f32(tr), f32(ts), f32(r2), f32(cc), f32(sc), f32(ones_bd)


def kernel(x, mix_norm_g, w_in, q_norm_g, k_norm_g, w_fourier, w_out, mlp_norm_g, w_up,
           w_down, final_norm_g):
    batch, seq, d_model = x.shape
    assert d_model == D_MODEL and seq % TOKEN_TILE == 0 and seq % Q_TILE == 0 and seq % IN_TILE == 0
    x2 = x.reshape(batch * seq, d_model)

    m1_np, tr_np, ts_np, r2_np, cc_np, sc_np, ones_np = _dft_constants(seq)
    m1 = jnp.asarray(m1_np).astype(BF16)
    r2 = jnp.asarray(r2_np).astype(BF16)
    tr, ts, cc, sc = (jnp.asarray(a) for a in (tr_np, ts_np, cc_np, sc_np))
    ones_bd = jnp.asarray(ones_np, dtype=BF16)
    cos_t, sin_a, sin_b = _rope_tables(seq)

    row = lambda g: g.reshape(1, -1).astype(F32)
    two_heads = lambda g: jnp.tile(g.astype(F32), V7X_LANES // HEAD_DIM).reshape(1, V7X_LANES)
    wbd = (jnp.eye(N_GROUPS, dtype=F32)[:, None, :, None]
           * w_fourier.astype(F32)[:, :, None, :]).reshape(FOURIER_WIDTH, FOURIER_WIDTH)

    qt, kx, vxt, u = _in_proj(x2, row(mix_norm_g), w_in.astype(BF16), ones_bd,
                            two_heads(q_norm_g), two_heads(k_norm_g),
                            cos_t, sin_a, sin_b, batch, seq)
    attn = _attention(qt, kx, vxt, batch, seq)
    q = _dft_stage1(u.reshape(batch, DFT_N1, DFT_N2 * FOURIER_WIDTH), m1, tr, ts)
    four = _dft_stage2(q.reshape(batch, 2, DFT_N1, DFT_N2, FOURIER_WIDTH), r2, cc, sc, wbd)
    four = four.reshape(batch * seq, FOURIER_WIDTH)
    out = _out_mlp(attn, four, x2, w_out.astype(BF16), row(mlp_norm_g),
                   w_up.astype(BF16), w_down.astype(BF16), row(final_norm_g))
    return out.reshape(batch, seq, d_model)
```

```python
import functools

import numpy as np
import jax
import jax.numpy as jnp
from jax import lax
from jax.experimental import pallas as pl
from jax.experimental.pallas import tpu as pltpu

D_MODEL = 1024
HEAD_DIM = 64
N_Q_HEADS = 8
N_KV_HEADS = 2
ATTN_WIDTH = N_Q_HEADS * HEAD_DIM
KV_WIDTH = N_KV_HEADS * HEAD_DIM
N_GROUPS = 8
GROUP_DIM = 64
FOURIER_WIDTH = N_GROUPS * GROUP_DIM
IN_PROJ_WIDTH = ATTN_WIDTH + 2 * KV_WIDTH + FOURIER_WIDTH
D_FF = 4 * D_MODEL
GRID_W = 64
AXIS_DIM = HEAD_DIM // 2
ROPE_THETA = 10000.0
NORM_EPS = 1e-6
LOG2_E = 1.4426950408889634

V7X_LANES = 128
V7X_VMEM_BYTES = 64 * 1024 * 1024

TOKEN_TILE = 512
IN_TILE = 2048
IN_SUB = 256
Q_TILE = 512
FF_CHUNK = 1024
DFT_N1 = 32
DFT_N2 = 64
DFT_K1_GROUP = 4
VMEM_LIMIT = 56 * 1024 * 1024

BF16 = jnp.bfloat16
F32 = jnp.float32


def _resident(shape):
    zeros = (0,) * len(shape)
    return pl.BlockSpec(shape, lambda *_: zeros, pipeline_mode=pl.Buffered(1))


def _rms_scale(x):
    return lax.rsqrt(jnp.mean(x * x, axis=-1, keepdims=True) + NORM_EPS)


def _in_proj_kernel(x_ref, g_ref, w_ref, ones_ref, qg_ref, kg_ref, cos_ref, sa_ref,
                    sb_ref, qt_ref, kx_ref, vxt_ref, u_ref):
    g = g_ref[...]
    ones = ones_ref[...]
    qg = qg_ref[...]
    kg = kg_ref[...]

    def project(i):
        x = x_ref[i * IN_SUB:(i + 1) * IN_SUB, :]
        h = x * _rms_scale(x) * g
        return jnp.dot(h.astype(BF16), w_ref[...], preferred_element_type=F32)

    def finish(i, proj):
        rows = slice(i * IN_SUB, (i + 1) * IN_SUB)
        cos = cos_ref[rows, :]
        sin_a = sa_ref[rows, :]
        sin_b = sb_ref[rows, :]

        def norm_rope(blk, gain, scale):
            ss = jnp.dot((blk * blk).astype(BF16), ones, preferred_element_type=F32)
            y = blk * (lax.rsqrt(ss * (1.0 / HEAD_DIM) + NORM_EPS) * scale) * gain
            return (y * cos + pltpu.roll(y, V7X_LANES - 16, 1) * sin_a
                    + pltpu.roll(y, 16, 1) * sin_b)

        for c in range(ATTN_WIDTH // V7X_LANES):
            sl = slice(c * V7X_LANES, (c + 1) * V7X_LANES)
            qt_ref[0, sl, rows] = norm_rope(
                proj[:, sl], qg, HEAD_DIM ** -0.5 * LOG2_E).T.astype(BF16)

        k = norm_rope(proj[:, ATTN_WIDTH:ATTN_WIDTH + KV_WIDTH], kg, 1.0)
        kr = pltpu.roll(k, HEAD_DIM, 1)
        lo = lax.broadcasted_iota(jnp.int32, k.shape, 1) < HEAD_DIM
        for v, blk in enumerate((jnp.where(lo, k, 0.0), jnp.where(lo, 0.0, kr),
                                 jnp.where(lo, kr, 0.0), jnp.where(lo, 0.0, k))):
            kx_ref[rows, v * V7X_LANES:(v + 1) * V7X_LANES] = blk.astype(BF16)

        vt = proj[:, ATTN_WIDTH + KV_WIDTH:ATTN_WIDTH + 2 * KV_WIDTH].T.astype(BF16)
        top, bot = vt[:HEAD_DIM], vt[HEAD_DIM:]
        one = jnp.ones_like(top)
        for v, (first, second) in enumerate(((top, one), (one, top), (bot, one), (one, bot))):
            vxt_ref[0, v * V7X_LANES:v * V7X_LANES + HEAD_DIM, rows] = first
            vxt_ref[0, v * V7X_LANES + HEAD_DIM:(v + 1) * V7X_LANES, rows] = second

        u_ref[rows, :] = proj[:, ATTN_WIDTH + 2 * KV_WIDTH:].astype(BF16)

    n_sub = IN_TILE // IN_SUB
    nxt = project(0)
    for i in range(n_sub):
        cur = nxt
        if i + 1 < n_sub:
            nxt = project(i + 1)
        finish(i, cur)


def _in_proj(x2, mix_g, w_in, ones_bd, qg, kg, cos_t, sin_a, sin_b, batch, seq):
    n = x2.shape[0]
    tiles_per_seq = seq // IN_TILE
    tok = lambda i: (i, 0)
    pos = lambda i: (i % tiles_per_seq, 0)
    chan_major = lambda i: (i // tiles_per_seq, 0, i % tiles_per_seq)
    return pl.pallas_call(
        _in_proj_kernel,
        grid=(n // IN_TILE,),
        in_specs=[
            pl.BlockSpec((IN_TILE, D_MODEL), tok),
            _resident((1, D_MODEL)),
            _resident((D_MODEL, IN_PROJ_WIDTH)),
            _resident((V7X_LANES, V7X_LANES)),
            _resident((1, V7X_LANES)),
            _resident((1, V7X_LANES)),
            pl.BlockSpec((IN_TILE, V7X_LANES), pos),
            pl.BlockSpec((IN_TILE, V7X_LANES), pos),
            pl.BlockSpec((IN_TILE, V7X_LANES), pos),
        ],
        out_specs=[
            pl.BlockSpec((1, ATTN_WIDTH, IN_TILE), chan_major),
            pl.BlockSpec((IN_TILE, 4 * V7X_LANES), tok),
            pl.BlockSpec((1, 4 * V7X_LANES, IN_TILE), chan_major),
            pl.BlockSpec((IN_TILE, FOURIER_WIDTH), tok),
        ],
        out_shape=[
            jax.ShapeDtypeStruct((batch, ATTN_WIDTH, seq), BF16),
            jax.ShapeDtypeStruct((n, 4 * V7X_LANES), BF16),
            jax.ShapeDtypeStruct((batch, 4 * V7X_LANES, seq), BF16),
            jax.ShapeDtypeStruct((n, FOURIER_WIDTH), BF16),
        ],
        compiler_params=pltpu.CompilerParams(
            dimension_semantics=("arbitrary",), vmem_limit_bytes=VMEM_LIMIT),
        name="in_proj",
    )(x2, mix_g, w_in, ones_bd, qg, kg, cos_t, sin_a, sin_b)


def _attn_kernel(qt_ref, kx_ref, vxt_ref, o_ref):
    top = lax.broadcasted_iota(jnp.int32, (V7X_LANES, Q_TILE), 0) < HEAD_DIM

    def variant(h):
        v = 2 * (h // (N_Q_HEADS // N_KV_HEADS)) + h % 2
        return slice(v * V7X_LANES, (v + 1) * V7X_LANES)

    def scores(h):
        c = h // 2
        qc = qt_ref[0, c * V7X_LANES:(c + 1) * V7X_LANES, :]
        return jnp.dot(kx_ref[:, variant(h)], qc, preferred_element_type=F32)

    s_next = scores(0)
    halves = []
    for h in range(N_Q_HEADS):
        s = s_next
        if h + 1 < N_Q_HEADS:
            s_next = scores(h + 1)
        m = jnp.max(s, axis=0, keepdims=True)
        e = jnp.exp2(s - m).astype(BF16)
        r = jnp.dot(vxt_ref[0, variant(h), :], e, preferred_element_type=F32)
        halves.append(r / jnp.concatenate([r[HEAD_DIM:], r[:HEAD_DIM]], axis=0))
        if h % 2 == 1:
            c = h // 2
            pair = jnp.where(top, halves[0], halves[1])
            o_ref[:, c * V7X_LANES:(c + 1) * V7X_LANES] = pair.T.astype(BF16)
            halves = []


def _attention(qt, kx, vxt, batch, seq):
    nq = seq // Q_TILE
    return pl.pallas_call(
        _attn_kernel,
        grid=(batch, nq),
        in_specs=[
            pl.BlockSpec((1, ATTN_WIDTH, Q_TILE), lambda b, i: (b, 0, i)),
            pl.BlockSpec((seq, 4 * V7X_LANES), lambda b, i: (b, 0)),
            pl.BlockSpec((1, 4 * V7X_LANES, seq), lambda b, i: (b, 0, 0)),
        ],
        out_specs=pl.BlockSpec((Q_TILE, ATTN_WIDTH), lambda b, i: (b * nq + i, 0)),
        out_shape=jax.ShapeDtypeStruct((batch * seq, ATTN_WIDTH), BF16),
        compiler_params=pltpu.CompilerParams(
            dimension_semantics=("arbitrary", "arbitrary"), vmem_limit_bytes=VMEM_LIMIT),
        name="attention",
    )(qt, kx, vxt)


def _fourier_kernel(u_ref, m1_ref, tr_ref, ts_ref, r2_ref, cc_ref, sc_ref, wbd_ref, y_ref,
                    ma_ref, mb_ref):
    @pl.when(pl.program_id(0) == 0)
    def _():
        w = wbd_ref[...]
        ma_ref[...] = jnp.dot(cc_ref[...], w, precision=lax.Precision.HIGHEST,
                              preferred_element_type=F32).astype(BF16)
        mb_ref[...] = jnp.dot(sc_ref[...], w, precision=lax.Precision.HIGHEST,
                              preferred_element_type=F32).astype(BF16)

    x = u_ref[...].astype(F32).reshape(DFT_N1, DFT_N2, FOURIER_WIDTH)
    xt = jnp.swapaxes(x, 0, 1).astype(BF16)

    m1 = m1_ref[...]
    q = []
    for n2 in range(DFT_N2):
        p = jnp.dot(m1, xt[n2], preferred_element_type=F32)
        pr, pi = p[:DFT_N1], p[DFT_N1:]
        tr, ts = tr_ref[n2], ts_ref[n2]
        q.append(jnp.concatenate([pr * tr + pi * ts, pi * tr - pr * ts], axis=0))
    q = jnp.swapaxes(jnp.stack(q, axis=0), 0, 1).astype(BF16)

    r2 = r2_ref[...]
    half = FOURIER_WIDTH // 2
    y = []
    for g in range(DFT_N1 // DFT_K1_GROUP):
        xr, xi = [], []
        for k1 in range(g * DFT_K1_GROUP, (g + 1) * DFT_K1_GROUP):
            qk = jnp.concatenate([q[k1], q[DFT_N1 + k1]], axis=0)
            xk = jnp.dot(r2, qk, preferred_element_type=F32)
            xr.append(xk[:DFT_N2].astype(BF16))
            xi.append(xk[DFT_N2:].astype(BF16))
        xr = jnp.concatenate(xr, axis=0)
        xi = jnp.concatenate(xi, axis=0)
        y.append(jnp.concatenate(
            [jnp.dot(xr[:, lanes], ma_ref[lanes, lanes], preferred_element_type=F32)
             + jnp.dot(xi[:, lanes], mb_ref[lanes, lanes], preferred_element_type=F32)
             for lanes in (slice(0, half), slice(half, FOURIER_WIDTH))], axis=1))
    y = jnp.concatenate(y, axis=0).reshape(DFT_N1, DFT_N2, FOURIER_WIDTH)
    y_ref[...] = jnp.swapaxes(y, 0, 1).reshape(DFT_N1 * DFT_N2, FOURIER_WIDTH).astype(BF16)


def _fourier(u, m1, tr, ts, r2, cc, sc, wbd, batch, seq):
    return pl.pallas_call(
        _fourier_kernel,
        grid=(batch,),
        in_specs=[
            pl.BlockSpec((seq, FOURIER_WIDTH), lambda b: (b, 0)),
            _resident((2 * DFT_N1, DFT_N1)),
            _resident((DFT_N2, DFT_N1, FOURIER_WIDTH)),
            _resident((DFT_N2, DFT_N1, FOURIER_WIDTH)),
            _resident((2 * DFT_N2, 2 * DFT_N2)),
            _resident((FOURIER_WIDTH, FOURIER_WIDTH)),
            _resident((FOURIER_WIDTH, FOURIER_WIDTH)),
            _resident((FOURIER_WIDTH, FOURIER_WIDTH)),
        ],
        out_specs=pl.BlockSpec((seq, FOURIER_WIDTH), lambda b: (b, 0)),
        out_shape=jax.ShapeDtypeStruct((batch * seq, FOURIER_WIDTH), BF16),
        scratch_shapes=[pltpu.VMEM((FOURIER_WIDTH, FOURIER_WIDTH), BF16),
                        pltpu.VMEM((FOURIER_WIDTH, FOURIER_WIDTH), BF16)],
        compiler_params=pltpu.CompilerParams(
            dimension_semantics=("arbitrary",), vmem_limit_bytes=VMEM_LIMIT),
        name="fourier",
    )(u, m1, tr, ts, r2, cc, sc, wbd)


def _out_mlp_kernel(a_ref, f_ref, x_ref, wo_ref, g2_ref, wu_ref, wd_ref, g3_ref, o_ref):
    mix = (jnp.dot(a_ref[...], wo_ref[:ATTN_WIDTH, :], preferred_element_type=F32)
           + jnp.dot(f_ref[...], wo_ref[ATTN_WIDTH:, :], preferred_element_type=F32))
    x1 = x_ref[...] + mix
    h = (x1 * _rms_scale(x1) * g2_ref[...]).astype(BF16)
    acc = x1
    for c in range(D_FF // FF_CHUNK):
        cs = slice(c * FF_CHUNK, (c + 1) * FF_CHUNK)
        z = jnp.dot(h, wu_ref[:, cs], preferred_element_type=F32)
        z = jnp.square(jnp.maximum(z, 0.0)).astype(BF16)
        acc = acc + jnp.dot(z, wd_ref[cs, :], preferred_element_type=F32)
    o_ref[...] = acc * _rms_scale(acc) * g3_ref[...]


def _out_mlp(attn, four, x2, w_out, g2, w_up, w_down, g3):
    n = x2.shape[0]
    tok = lambda i: (i, 0)
    return pl.pallas_call(
        _out_mlp_kernel,
        grid=(n // TOKEN_TILE,),
        in_specs=[
            pl.BlockSpec((TOKEN_TILE, ATTN_WIDTH), tok),
            pl.BlockSpec((TOKEN_TILE, FOURIER_WIDTH), tok),
            pl.BlockSpec((TOKEN_TILE, D_MODEL), tok),
            _resident((D_MODEL, D_MODEL)),
            _resident((1, D_MODEL)),
            _resident((D_MODEL, D_FF)),
            _resident((D_FF, D_MODEL)),
            _resident((1, D_MODEL)),
        ],
        out_specs=pl.BlockSpec((TOKEN_TILE, D_MODEL), tok),
        out_shape=jax.ShapeDtypeStruct((n, D_MODEL), F32),
        compiler_params=pltpu.CompilerParams(
            dimension_semantics=("arbitrary",), vmem_limit_bytes=VMEM_LIMIT),
        name="out_mlp",
    )(attn, four, x2, w_out, g2, w_up, w_down, g3)


def _rope_tables(seq):
    rows = seq // GRID_W
    row = jnp.repeat(jnp.arange(rows, dtype=jnp.int32), GRID_W)
    col = jnp.tile(jnp.arange(GRID_W, dtype=jnp.int32), rows)
    inv_freq = ROPE_THETA ** (-jnp.arange(0, AXIS_DIM, 2, dtype=F32) / AXIS_DIM)
    row_ang = row.astype(F32)[:, None] * inv_freq[None, :]
    col_ang = col.astype(F32)[:, None] * inv_freq[None, :]
    ang = jnp.concatenate([row_ang, row_ang, col_ang, col_ang], axis=-1)
    ang = jnp.concatenate([ang, ang], axis=-1)
    first_half = (jnp.arange(V7X_LANES) % AXIS_DIM) < (AXIS_DIM // 2)
    sin = jnp.sin(ang)
    return (jnp.cos(ang), jnp.where(first_half, -sin, 0.0), jnp.where(first_half, 0.0, sin))


@functools.lru_cache(maxsize=None)
def _dft_constants(seq):
    assert seq == DFT_N1 * DFT_N2
    k1 = np.arange(DFT_N1, dtype=np.int64)
    a1 = 2.0 * np.pi * ((k1[:, None] * k1[None, :]) % DFT_N1) / DFT_N1
    m1 = np.concatenate([np.cos(a1), -np.sin(a1)], axis=0)
    n2 = np.arange(DFT_N2, dtype=np.int64)
    at = 2.0 * np.pi * (n2[:, None] * k1[None, :]) / seq
    lanes = np.ones((1, 1, FOURIER_WIDTH))
    tr = np.cos(at)[:, :, None] * lanes
    ts = np.sin(at)[:, :, None] * lanes
    a2 = 2.0 * np.pi * ((n2[:, None] * n2[None, :]) % DFT_N2) / DFT_N2
    c2, s2 = np.cos(a2), np.sin(a2)
    r2 = np.block([[c2, s2], [-s2, c2]])
    c = np.arange(GROUP_DIM, dtype=np.int64)
    angc = 2.0 * np.pi * ((c[:, None] * c[None, :]) % GROUP_DIM) / GROUP_DIM
    scale = 1.0 / np.sqrt(float(seq * GROUP_DIM))
    eye = np.eye(N_GROUPS)
    cc = np.kron(eye, np.cos(angc)) * scale
    sc = np.kron(eye, np.sin(angc)) * scale
    ones_bd = np.kron(np.eye(V7X_LANES // HEAD_DIM), np.ones((HEAD_DIM, HEAD_DIM)))
    f32 = lambda a: a.astype(np.float32)
    return f32(m1), f32(tr), f32(ts), f32(r2), f32(cc), f32(sc), f32(ones_bd)


def kernel(x, mix_norm_g, w_in, q_norm_g, k_norm_g, w_fourier, w_out, mlp_norm_g, w_up,
           w_down, final_norm_g):
    batch, seq, d_model = x.shape
    assert d_model == D_MODEL and seq % TOKEN_TILE == 0 and seq % Q_TILE == 0 and seq % IN_TILE == 0
    x2 = x.reshape(batch * seq, d_model)

    m1_np, tr_np, ts_np, r2_np, cc_np, sc_np, ones_np = _dft_constants(seq)
    m1 = jnp.asarray(m1_np).astype(BF16)
    r2 = jnp.asarray(r2_np).astype(BF16)
    tr, ts, cc, sc = (jnp.asarray(a) for a in (tr_np, ts_np, cc_np, sc_np))
    ones_bd = jnp.asarray(ones_np, dtype=BF16)
    cos_t, sin_a, sin_b = _rope_tables(seq)

    row = lambda g: g.reshape(1, -1).astype(F32)
    two_heads = lambda g: jnp.tile(g.astype(F32), V7X_LANES // HEAD_DIM).reshape(1, V7X_LANES)
    wbd = (jnp.eye(N_GROUPS, dtype=F32)[:, None, :, None]
           * w_fourier.astype(F32)[:, :, None, :]).reshape(FOURIER_WIDTH, FOURIER_WIDTH)

    qt, kx, vxt, u = _in_proj(x2, row(mix_norm_g), w_in.astype(BF16), ones_bd,
                            two_heads(q_norm_g), two_heads(k_norm_g),
                            cos_t, sin_a, sin_b, batch, seq)
    attn = _attention(qt, kx, vxt, batch, seq)
    four = _fourier(u, m1, tr, ts, r2, cc, sc, wbd, batch, seq)
    out = _out_mlp(attn, four, x2, w_out.astype(BF16), row(mlp_norm_g),
                   w_up.astype(BF16), w_down.astype(BF16), row(final_norm_g))
    return out.reshape(batch, seq, d_model)
```

```python
import functools

import numpy as np
import jax
import jax.numpy as jnp
from jax import lax
from jax.experimental import pallas as pl
from jax.experimental.pallas import tpu as pltpu

D_MODEL = 1024
HEAD_DIM = 64
N_Q_HEADS = 8
N_KV_HEADS = 2
ATTN_WIDTH = N_Q_HEADS * HEAD_DIM
KV_WIDTH = N_KV_HEADS * HEAD_DIM
N_GROUPS = 8
GROUP_DIM = 64
FOURIER_WIDTH = N_GROUPS * GROUP_DIM
IN_PROJ_WIDTH = ATTN_WIDTH + 2 * KV_WIDTH + FOURIER_WIDTH
D_FF = 4 * D_MODEL
GRID_W = 64
AXIS_DIM = HEAD_DIM // 2
ROPE_THETA = 10000.0
NORM_EPS = 1e-6
LOG2_E = 1.4426950408889634

V7X_LANES = 128
V7X_VMEM_BYTES = 64 * 1024 * 1024

TOKEN_TILE = 512
IN_TILE = 2048
IN_SUB = 256
Q_TILE = 512
Q_TILES_PER_STEP = 2
FF_CHUNK = 1024
DFT_N1 = 32
DFT_N2 = 64
DFT_K1_GROUP = 4
VMEM_LIMIT = 56 * 1024 * 1024

BF16 = jnp.bfloat16
F32 = jnp.float32


def _resident(shape):
    zeros = (0,) * len(shape)
    return pl.BlockSpec(shape, lambda *_: zeros, pipeline_mode=pl.Buffered(1))


def _rms_scale(x):
    return lax.rsqrt(jnp.mean(x * x, axis=-1, keepdims=True) + NORM_EPS)


def _in_proj_kernel(x_ref, g_ref, w_ref, ones_ref, qg_ref, kg_ref, cos_ref, sa_ref,
                    sb_ref, qt_ref, kx_ref, vxt_ref, u_ref):
    g = g_ref[...]
    ones = ones_ref[...]
    qg = qg_ref[...]
    kg = kg_ref[...]
    w = w_ref[...].astype(BF16)

    def project(i):
        x = x_ref[i * IN_SUB:(i + 1) * IN_SUB, :]
        h = x * _rms_scale(x) * g
        return jnp.dot(h.astype(BF16), w, preferred_element_type=F32)

    def finish(i, proj):
        rows = slice(i * IN_SUB, (i + 1) * IN_SUB)
        cos = cos_ref[rows, :]
        sin_a = sa_ref[rows, :]
        sin_b = sb_ref[rows, :]

        def norm_rope(blk, gain, scale):
            ss = jnp.dot((blk * blk).astype(BF16), ones, preferred_element_type=F32)
            y = blk * (lax.rsqrt(ss * (1.0 / HEAD_DIM) + NORM_EPS) * scale) * gain
            return (y * cos + pltpu.roll(y, V7X_LANES - 16, 1) * sin_a
                    + pltpu.roll(y, 16, 1) * sin_b)

        for c in range(ATTN_WIDTH // V7X_LANES):
            sl = slice(c * V7X_LANES, (c + 1) * V7X_LANES)
            qt_ref[0, sl, rows] = norm_rope(
                proj[:, sl], qg, HEAD_DIM ** -0.5 * LOG2_E).T.astype(BF16)

        k = norm_rope(proj[:, ATTN_WIDTH:ATTN_WIDTH + KV_WIDTH], kg, 1.0)
        kr = pltpu.roll(k, HEAD_DIM, 1)
        lo = lax.broadcasted_iota(jnp.int32, k.shape, 1) < HEAD_DIM
        for v, blk in enumerate((jnp.where(lo, k, 0.0), jnp.where(lo, 0.0, kr),
                                 jnp.where(lo, kr, 0.0), jnp.where(lo, 0.0, k))):
            kx_ref[rows, v * V7X_LANES:(v + 1) * V7X_LANES] = blk.astype(BF16)

        vt = proj[:, ATTN_WIDTH + KV_WIDTH:ATTN_WIDTH + 2 * KV_WIDTH].T.astype(BF16)
        top, bot = vt[:HEAD_DIM], vt[HEAD_DIM:]
        one = jnp.ones_like(top)
        for v, (first, second) in enumerate(((top, one), (one, top), (bot, one), (one, bot))):
            vxt_ref[0, v * V7X_LANES:v * V7X_LANES + HEAD_DIM, rows] = first
            vxt_ref[0, v * V7X_LANES + HEAD_DIM:(v + 1) * V7X_LANES, rows] = second

        u_ref[rows, :] = proj[:, ATTN_WIDTH + 2 * KV_WIDTH:].astype(BF16)

    n_sub = IN_TILE // IN_SUB
    nxt = project(0)
    for i in range(n_sub):
        cur = nxt
        if i + 1 < n_sub:
            nxt = project(i + 1)
        finish(i, cur)


def _in_proj(x2, mix_g, w_in, ones_bd, qg, kg, cos_t, sin_a, sin_b, batch, seq):
    n = x2.shape[0]
    tiles_per_seq = seq // IN_TILE
    tok = lambda i: (i, 0)
    pos = lambda i: (i % tiles_per_seq, 0)
    chan_major = lambda i: (i // tiles_per_seq, 0, i % tiles_per_seq)
    return pl.pallas_call(
        _in_proj_kernel,
        grid=(n // IN_TILE,),
        in_specs=[
            pl.BlockSpec((IN_TILE, D_MODEL), tok),
            _resident((1, D_MODEL)),
            _resident((D_MODEL, IN_PROJ_WIDTH)),
            _resident((V7X_LANES, V7X_LANES)),
            _resident((1, V7X_LANES)),
            _resident((1, V7X_LANES)),
            pl.BlockSpec((IN_TILE, V7X_LANES), pos),
            pl.BlockSpec((IN_TILE, V7X_LANES), pos),
            pl.BlockSpec((IN_TILE, V7X_LANES), pos),
        ],
        out_specs=[
            pl.BlockSpec((1, ATTN_WIDTH, IN_TILE), chan_major),
            pl.BlockSpec((IN_TILE, 4 * V7X_LANES), tok),
            pl.BlockSpec((1, 4 * V7X_LANES, IN_TILE), chan_major),
            pl.BlockSpec((IN_TILE, FOURIER_WIDTH), tok),
        ],
        out_shape=[
            jax.ShapeDtypeStruct((batch, ATTN_WIDTH, seq), BF16),
            jax.ShapeDtypeStruct((n, 4 * V7X_LANES), BF16),
            jax.ShapeDtypeStruct((batch, 4 * V7X_LANES, seq), BF16),
            jax.ShapeDtypeStruct((n, FOURIER_WIDTH), BF16),
        ],
        compiler_params=pltpu.CompilerParams(
            dimension_semantics=("arbitrary",), vmem_limit_bytes=VMEM_LIMIT),
        name="in_proj",
    )(x2, mix_g, w_in, ones_bd, qg, kg, cos_t, sin_a, sin_b)


def _attn_kernel(qt_ref, kx_ref, vxt_ref, o_ref):
    top = lax.broadcasted_iota(jnp.int32, (V7X_LANES, Q_TILE), 0) < HEAD_DIM

    def variant(h):
        v = 2 * (h // (N_Q_HEADS // N_KV_HEADS)) + h % 2
        return slice(v * V7X_LANES, (v + 1) * V7X_LANES)

    def scores(item):
        t, h = divmod(item, N_Q_HEADS)
        c = h // 2
        qc = qt_ref[0, c * V7X_LANES:(c + 1) * V7X_LANES, t * Q_TILE:(t + 1) * Q_TILE]
        return jnp.dot(kx_ref[:, variant(h)], qc, preferred_element_type=F32)

    n_items = Q_TILES_PER_STEP * N_Q_HEADS
    s_next = scores(0)
    halves = []
    for item in range(n_items):
        t, h = divmod(item, N_Q_HEADS)
        s = s_next
        if item + 1 < n_items:
            s_next = scores(item + 1)
        m = jnp.max(s, axis=0, keepdims=True)
        e = jnp.exp2(s - m).astype(BF16)
        r = jnp.dot(vxt_ref[0, variant(h), :], e, preferred_element_type=F32)
        halves.append(r / jnp.concatenate([r[HEAD_DIM:], r[:HEAD_DIM]], axis=0))
        if h % 2 == 1:
            c = h // 2
            pair = jnp.where(top, halves[0], halves[1])
            o_ref[t * Q_TILE:(t + 1) * Q_TILE, c * V7X_LANES:(c + 1) * V7X_LANES] = (
                pair.T.astype(BF16))
            halves = []


def _attention(qt, kx, vxt, batch, seq):
    q_step = Q_TILE * Q_TILES_PER_STEP
    nq = seq // q_step
    return pl.pallas_call(
        _attn_kernel,
        grid=(batch, nq),
        in_specs=[
            pl.BlockSpec((1, ATTN_WIDTH, q_step), lambda b, i: (b, 0, i)),
            pl.BlockSpec((seq, 4 * V7X_LANES), lambda b, i: (b, 0)),
            pl.BlockSpec((1, 4 * V7X_LANES, seq), lambda b, i: (b, 0, 0)),
        ],
        out_specs=pl.BlockSpec((q_step, ATTN_WIDTH), lambda b, i: (b * nq + i, 0)),
        out_shape=jax.ShapeDtypeStruct((batch * seq, ATTN_WIDTH), BF16),
        compiler_params=pltpu.CompilerParams(
            dimension_semantics=("arbitrary", "arbitrary"), vmem_limit_bytes=VMEM_LIMIT),
        name="attention",
    )(qt, kx, vxt)


def _fourier_kernel(u_ref, m1_ref, tr_ref, ts_ref, r2_ref, cc_ref, sc_ref, wbd_ref, y_ref,
                    ma_ref, mb_ref):
    @pl.when(pl.program_id(0) == 0)
    def _():
        w = wbd_ref[...]
        ma_ref[...] = jnp.dot(cc_ref[...], w, precision=lax.Precision.HIGHEST,
                              preferred_element_type=F32).astype(BF16)
        mb_ref[...] = jnp.dot(sc_ref[...], w, precision=lax.Precision.HIGHEST,
                              preferred_element_type=F32).astype(BF16)

    x = u_ref[...].astype(F32).reshape(DFT_N1, DFT_N2, FOURIER_WIDTH)
    xt = jnp.swapaxes(x, 0, 1).astype(BF16)

    m1 = m1_ref[...]
    q = []
    for n2 in range(DFT_N2):
        p = jnp.dot(m1, xt[n2], preferred_element_type=F32)
        pr, pi = p[:DFT_N1], p[DFT_N1:]
        tr, ts = tr_ref[n2], ts_ref[n2]
        q.append(jnp.concatenate([pr * tr + pi * ts, pi * tr - pr * ts], axis=0))
    q = jnp.swapaxes(jnp.stack(q, axis=0), 0, 1).astype(BF16)

    r2 = r2_ref[...]
    half = FOURIER_WIDTH // 2
    y = []
    for g in range(DFT_N1 // DFT_K1_GROUP):
        xr, xi = [], []
        for k1 in range(g * DFT_K1_GROUP, (g + 1) * DFT_K1_GROUP):
            qk = jnp.concatenate([q[k1], q[DFT_N1 + k1]], axis=0)
            xk = jnp.dot(r2, qk, preferred_element_type=F32)
            xr.append(xk[:DFT_N2].astype(BF16))
            xi.append(xk[DFT_N2:].astype(BF16))
        xr = jnp.concatenate(xr, axis=0)
        xi = jnp.concatenate(xi, axis=0)
        y.append(jnp.concatenate(
            [jnp.dot(xr[:, lanes], ma_ref[lanes, lanes], preferred_element_type=F32)
             + jnp.dot(xi[:, lanes], mb_ref[lanes, lanes], preferred_element_type=F32)
             for lanes in (slice(0, half), slice(half, FOURIER_WIDTH))], axis=1))
    y = jnp.concatenate(y, axis=0).reshape(DFT_N1, DFT_N2, FOURIER_WIDTH)
    y_ref[...] = jnp.swapaxes(y, 0, 1).reshape(DFT_N1 * DFT_N2, FOURIER_WIDTH).astype(BF16)


def _fourier(u, m1, tr, ts, r2, cc, sc, wbd, batch, seq):
    return pl.pallas_call(
        _fourier_kernel,
        grid=(batch,),
        in_specs=[
            pl.BlockSpec((seq, FOURIER_WIDTH), lambda b: (b, 0)),
            _resident((2 * DFT_N1, DFT_N1)),
            _resident((DFT_N2, DFT_N1, FOURIER_WIDTH)),
            _resident((DFT_N2, DFT_N1, FOURIER_WIDTH)),
            _resident((2 * DFT_N2, 2 * DFT_N2)),
            _resident((FOURIER_WIDTH, FOURIER_WIDTH)),
            _resident((FOURIER_WIDTH, FOURIER_WIDTH)),
            _resident((FOURIER_WIDTH, FOURIER_WIDTH)),
        ],
        out_specs=pl.BlockSpec((seq, FOURIER_WIDTH), lambda b: (b, 0)),
        out_shape=jax.ShapeDtypeStruct((batch * seq, FOURIER_WIDTH), BF16),
        scratch_shapes=[pltpu.VMEM((FOURIER_WIDTH, FOURIER_WIDTH), BF16),
                        pltpu.VMEM((FOURIER_WIDTH, FOURIER_WIDTH), BF16)],
        compiler_params=pltpu.CompilerParams(
            dimension_semantics=("arbitrary",), vmem_limit_bytes=VMEM_LIMIT),
        name="fourier",
    )(u, m1, tr, ts, r2, cc, sc, wbd)


def _out_mlp_kernel(a_ref, f_ref, x_ref, wo_ref, g2_ref, wu_ref, wd_ref, g3_ref, o_ref):
    mix = (jnp.dot(a_ref[...], wo_ref[:ATTN_WIDTH, :].astype(BF16), preferred_element_type=F32)
           + jnp.dot(f_ref[...], wo_ref[ATTN_WIDTH:, :].astype(BF16), preferred_element_type=F32))
    x1 = x_ref[...] + mix
    h = (x1 * _rms_scale(x1) * g2_ref[...]).astype(BF16)
    acc = x1
    for c in range(D_FF // FF_CHUNK):
        cs = slice(c * FF_CHUNK, (c + 1) * FF_CHUNK)
        z = jnp.dot(h, wu_ref[:, cs], preferred_element_type=F32)
        z = jnp.square(jnp.maximum(z, 0.0)).astype(BF16)
        acc = acc + jnp.dot(z, wd_ref[cs, :], preferred_element_type=F32)
    o_ref[...] = acc * _rms_scale(acc) * g3_ref[...]


def _out_mlp(attn, four, x2, w_out, g2, w_up, w_down, g3):
    n = x2.shape[0]
    tok = lambda i: (i, 0)
    return pl.pallas_call(
        _out_mlp_kernel,
        grid=(n // TOKEN_TILE,),
        in_specs=[
            pl.BlockSpec((TOKEN_TILE, ATTN_WIDTH), tok),
            pl.BlockSpec((TOKEN_TILE, FOURIER_WIDTH), tok),
            pl.BlockSpec((TOKEN_TILE, D_MODEL), tok),
            _resident((D_MODEL, D_MODEL)),
            _resident((1, D_MODEL)),
            _resident((D_MODEL, D_FF)),
            _resident((D_FF, D_MODEL)),
            _resident((1, D_MODEL)),
        ],
        out_specs=pl.BlockSpec((TOKEN_TILE, D_MODEL), tok),
        out_shape=jax.ShapeDtypeStruct((n, D_MODEL), F32),
        compiler_params=pltpu.CompilerParams(
            dimension_semantics=("arbitrary",), vmem_limit_bytes=VMEM_LIMIT),
        name="out_mlp",
    )(attn, four, x2, w_out, g2, w_up, w_down, g3)


@functools.lru_cache(maxsize=None)
def _rope_tables(seq):
    rows = seq // GRID_W
    row = np.repeat(np.arange(rows), GRID_W).astype(np.float64)
    col = np.tile(np.arange(GRID_W), rows).astype(np.float64)
    inv_freq = ROPE_THETA ** (-np.arange(0, AXIS_DIM, 2, dtype=np.float64) / AXIS_DIM)
    row_ang = row[:, None] * inv_freq[None, :]
    col_ang = col[:, None] * inv_freq[None, :]
    ang = np.concatenate([row_ang, row_ang, col_ang, col_ang], axis=-1)
    ang = np.concatenate([ang, ang], axis=-1)
    first_half = (np.arange(V7X_LANES) % AXIS_DIM) < (AXIS_DIM // 2)
    sin = np.sin(ang)
    f32 = lambda a: a.astype(np.float32)
    return (f32(np.cos(ang)), f32(np.where(first_half, -sin, 0.0)),
            f32(np.where(first_half, 0.0, sin)))


@functools.lru_cache(maxsize=None)
def _dft_constants(seq):
    assert seq == DFT_N1 * DFT_N2
    k1 = np.arange(DFT_N1, dtype=np.int64)
    a1 = 2.0 * np.pi * ((k1[:, None] * k1[None, :]) % DFT_N1) / DFT_N1
    m1 = np.concatenate([np.cos(a1), -np.sin(a1)], axis=0)
    n2 = np.arange(DFT_N2, dtype=np.int64)
    at = 2.0 * np.pi * (n2[:, None] * k1[None, :]) / seq
    lanes = np.ones((1, 1, FOURIER_WIDTH))
    tr = np.cos(at)[:, :, None] * lanes
    ts = np.sin(at)[:, :, None] * lanes
    a2 = 2.0 * np.pi * ((n2[:, None] * n2[None, :]) % DFT_N2) / DFT_N2
    c2, s2 = np.cos(a2), np.sin(a2)
    r2 = np.block([[c2, s2], [-s2, c2]])
    c = np.arange(GROUP_DIM, dtype=np.int64)
    angc = 2.0 * np.pi * ((c[:, None] * c[None, :]) % GROUP_DIM) / GROUP_DIM
    scale = 1.0 / np.sqrt(float(seq * GROUP_DIM))
    eye = np.eye(N_GROUPS)
    cc = np.kron(eye, np.cos(angc)) * scale
    sc = np.kron(eye, np.sin(angc)) * scale
    ones_bd = np.kron(np.eye(V7X_LANES // HEAD_DIM), np.ones((HEAD_DIM, HEAD_DIM)))
    f32 = lambda a: a.astype(np.float32)
    return f32(m1), f32(tr), f32(ts), f32(r2), f32(cc), f32(sc), f32(ones_bd)


def kernel(x, mix_norm_g, w_in, q_norm_g, k_norm_g, w_fourier, w_out, mlp_norm_g, w_up,
           w_down, final_norm_g):
    batch, seq, d_model = x.shape
    assert d_model == D_MODEL and seq % TOKEN_TILE == 0 and seq % Q_TILE == 0 and seq % IN_TILE == 0
    x2 = x.reshape(batch * seq, d_model)

    m1_np, tr_np, ts_np, r2_np, cc_np, sc_np, ones_np = _dft_constants(seq)
    m1 = jnp.asarray(m1_np).astype(BF16)
    r2 = jnp.asarray(r2_np).astype(BF16)
    tr, ts, cc, sc = (jnp.asarray(a) for a in (tr_np, ts_np, cc_np, sc_np))
    ones_bd = jnp.asarray(ones_np, dtype=BF16)
    cos_t, sin_a, sin_b = (jnp.asarray(a) for a in _rope_tables(seq))

    row = lambda g: g.reshape(1, -1).astype(F32)
    two_heads = lambda g: jnp.tile(g.astype(F32), V7X_LANES // HEAD_DIM).reshape(1, V7X_LANES)
    wbd = (jnp.eye(N_GROUPS, dtype=F32)[:, None, :, None]
           * w_fourier.astype(F32)[:, :, None, :]).reshape(FOURIER_WIDTH, FOURIER_WIDTH)

    qt, kx, vxt, u = _in_proj(x2, row(mix_norm_g), w_in.astype(F32), ones_bd,
                            two_heads(q_norm_g), two_heads(k_norm_g),
                            cos_t, sin_a, sin_b, batch, seq)
    attn = _attention(qt, kx, vxt, batch, seq)
    four = _fourier(u, m1, tr, ts, r2, cc, sc, wbd, batch, seq)
    out = _out_mlp(attn, four, x2, w_out.astype(F32), row(mlp_norm_g),
                   w_up.astype(BF16), w_down.astype(BF16), row(final_norm_g))
    return out.reshape(batch, seq, d_model)
```

```python
import functools

import numpy as np
import jax
import jax.numpy as jnp
from jax import lax
from jax.experimental import pallas as pl
from jax.experimental.pallas import tpu as pltpu

D_MODEL = 1024
HEAD_DIM = 64
N_Q_HEADS = 8
N_KV_HEADS = 2
ATTN_WIDTH = N_Q_HEADS * HEAD_DIM
KV_WIDTH = N_KV_HEADS * HEAD_DIM
N_GROUPS = 8
GROUP_DIM = 64
FOURIER_WIDTH = N_GROUPS * GROUP_DIM
IN_PROJ_WIDTH = ATTN_WIDTH + 2 * KV_WIDTH + FOURIER_WIDTH
D_FF = 4 * D_MODEL
GRID_W = 64
AXIS_DIM = HEAD_DIM // 2
ROPE_THETA = 10000.0
NORM_EPS = 1e-6
LOG2_E = 1.4426950408889634

V7X_LANES = 128
V7X_VMEM_BYTES = 64 * 1024 * 1024

TOKEN_TILE = 1024
MLP_SUB = 512
IN_TILE = 2048
IN_SUB = 256
Q_TILE = 512
Q_TILES_PER_STEP = 2
FF_CHUNK = 1024
DFT_N1 = 32
DFT_N2 = 64
DFT_K1_GROUP = 4
VMEM_LIMIT = 56 * 1024 * 1024

BF16 = jnp.bfloat16
F32 = jnp.float32


def _resident(shape):
    zeros = (0,) * len(shape)
    return pl.BlockSpec(shape, lambda *_: zeros, pipeline_mode=pl.Buffered(1))


def _rms_scale(x):
    return lax.rsqrt(jnp.mean(x * x, axis=-1, keepdims=True) + NORM_EPS)


def _in_proj_kernel(x_ref, g_ref, w_ref, ones_ref, qg_ref, kg_ref, cos_ref, sa_ref,
                    sb_ref, qt_ref, kx_ref, vxt_ref, u_ref):
    g = g_ref[...]
    ones = ones_ref[...]
    qg = qg_ref[...]
    kg = kg_ref[...]
    w = w_ref[...].astype(BF16)

    def project(i):
        x = x_ref[i * IN_SUB:(i + 1) * IN_SUB, :]
        h = x * _rms_scale(x) * g
        return jnp.dot(h.astype(BF16), w, preferred_element_type=F32)

    def finish(i, proj):
        rows = slice(i * IN_SUB, (i + 1) * IN_SUB)
        cos = cos_ref[rows, :]
        sin_a = sa_ref[rows, :]
        sin_b = sb_ref[rows, :]

        def norm_rope(blk, gain, scale):
            ss = jnp.dot((blk * blk).astype(BF16), ones, preferred_element_type=F32)
            y = blk * (lax.rsqrt(ss * (1.0 / HEAD_DIM) + NORM_EPS) * scale) * gain
            return (y * cos + pltpu.roll(y, V7X_LANES - 16, 1) * sin_a
                    + pltpu.roll(y, 16, 1) * sin_b)

        for c in range(ATTN_WIDTH // V7X_LANES):
            sl = slice(c * V7X_LANES, (c + 1) * V7X_LANES)
            qt_ref[0, sl, rows] = norm_rope(
                proj[:, sl], qg, HEAD_DIM ** -0.5 * LOG2_E).T.astype(BF16)

        k = norm_rope(proj[:, ATTN_WIDTH:ATTN_WIDTH + KV_WIDTH], kg, 1.0)
        kr = pltpu.roll(k, HEAD_DIM, 1)
        lo = lax.broadcasted_iota(jnp.int32, k.shape, 1) < HEAD_DIM
        for v, blk in enumerate((jnp.where(lo, k, 0.0), jnp.where(lo, 0.0, kr),
                                 jnp.where(lo, kr, 0.0), jnp.where(lo, 0.0, k))):
            kx_ref[rows, v * V7X_LANES:(v + 1) * V7X_LANES] = blk.astype(BF16)

        vt = proj[:, ATTN_WIDTH + KV_WIDTH:ATTN_WIDTH + 2 * KV_WIDTH].T.astype(BF16)
        top, bot = vt[:HEAD_DIM], vt[HEAD_DIM:]
        one = jnp.ones_like(top)
        for v, (first, second) in enumerate(((top, one), (one, top), (bot, one), (one, bot))):
            vxt_ref[0, v * V7X_LANES:v * V7X_LANES + HEAD_DIM, rows] = first
            vxt_ref[0, v * V7X_LANES + HEAD_DIM:(v + 1) * V7X_LANES, rows] = second

        u_ref[rows, :] = proj[:, ATTN_WIDTH + 2 * KV_WIDTH:].astype(BF16)

    n_sub = IN_TILE // IN_SUB
    nxt = project(0)
    for i in range(n_sub):
        cur = nxt
        if i + 1 < n_sub:
            nxt = project(i + 1)
        finish(i, cur)


def _in_proj(x2, mix_g, w_in, ones_bd, qg, kg, cos_t, sin_a, sin_b, batch, seq):
    n = x2.shape[0]
    tiles_per_seq = seq // IN_TILE
    tok = lambda i: (i, 0)
    pos = lambda i: (i % tiles_per_seq, 0)
    chan_major = lambda i: (i // tiles_per_seq, 0, i % tiles_per_seq)
    return pl.pallas_call(
        _in_proj_kernel,
        grid=(n // IN_TILE,),
        in_specs=[
            pl.BlockSpec((IN_TILE, D_MODEL), tok),
            _resident((1, D_MODEL)),
            _resident((D_MODEL, IN_PROJ_WIDTH)),
            _resident((V7X_LANES, V7X_LANES)),
            _resident((1, V7X_LANES)),
            _resident((1, V7X_LANES)),
            pl.BlockSpec((IN_TILE, V7X_LANES), pos),
            pl.BlockSpec((IN_TILE, V7X_LANES), pos),
            pl.BlockSpec((IN_TILE, V7X_LANES), pos),
        ],
        out_specs=[
            pl.BlockSpec((1, ATTN_WIDTH, IN_TILE), chan_major),
            pl.BlockSpec((IN_TILE, 4 * V7X_LANES), tok),
            pl.BlockSpec((1, 4 * V7X_LANES, IN_TILE), chan_major),
            pl.BlockSpec((IN_TILE, FOURIER_WIDTH), tok),
        ],
        out_shape=[
            jax.ShapeDtypeStruct((batch, ATTN_WIDTH, seq), BF16),
            jax.ShapeDtypeStruct((n, 4 * V7X_LANES), BF16),
            jax.ShapeDtypeStruct((batch, 4 * V7X_LANES, seq), BF16),
            jax.ShapeDtypeStruct((n, FOURIER_WIDTH), BF16),
        ],
        compiler_params=pltpu.CompilerParams(
            dimension_semantics=("arbitrary",), vmem_limit_bytes=VMEM_LIMIT),
        name="in_proj",
    )(x2, mix_g, w_in, ones_bd, qg, kg, cos_t, sin_a, sin_b)


def _attn_kernel(qt_ref, kx_ref, vxt_ref, o_ref):
    top = lax.broadcasted_iota(jnp.int32, (V7X_LANES, Q_TILE), 0) < HEAD_DIM

    def variant(h):
        v = 2 * (h // (N_Q_HEADS // N_KV_HEADS)) + h % 2
        return slice(v * V7X_LANES, (v + 1) * V7X_LANES)

    def scores(item):
        t, h = divmod(item, N_Q_HEADS)
        c = h // 2
        qc = qt_ref[0, c * V7X_LANES:(c + 1) * V7X_LANES, t * Q_TILE:(t + 1) * Q_TILE]
        return jnp.dot(kx_ref[:, variant(h)], qc, preferred_element_type=F32)

    n_items = Q_TILES_PER_STEP * N_Q_HEADS
    s_next = scores(0)
    halves = []
    for item in range(n_items):
        t, h = divmod(item, N_Q_HEADS)
        s = s_next
        if item + 1 < n_items:
            s_next = scores(item + 1)
        m = jnp.max(s, axis=0, keepdims=True)
        e = jnp.exp2(s - m).astype(BF16)
        r = jnp.dot(vxt_ref[0, variant(h), :], e, preferred_element_type=F32)
        halves.append(r / jnp.concatenate([r[HEAD_DIM:], r[:HEAD_DIM]], axis=0))
        if h % 2 == 1:
            c = h // 2
            pair = jnp.where(top, halves[0], halves[1])
            o_ref[t * Q_TILE:(t + 1) * Q_TILE, c * V7X_LANES:(c + 1) * V7X_LANES] = (
                pair.T.astype(BF16))
            halves = []


def _attention(qt, kx, vxt, batch, seq):
    q_step = Q_TILE * Q_TILES_PER_STEP
    nq = seq // q_step
    return pl.pallas_call(
        _attn_kernel,
        grid=(batch, nq),
        in_specs=[
            pl.BlockSpec((1, ATTN_WIDTH, q_step), lambda b, i: (b, 0, i)),
            pl.BlockSpec((seq, 4 * V7X_LANES), lambda b, i: (b, 0)),
            pl.BlockSpec((1, 4 * V7X_LANES, seq), lambda b, i: (b, 0, 0)),
        ],
        out_specs=pl.BlockSpec((q_step, ATTN_WIDTH), lambda b, i: (b * nq + i, 0)),
        out_shape=jax.ShapeDtypeStruct((batch * seq, ATTN_WIDTH), BF16),
        compiler_params=pltpu.CompilerParams(
            dimension_semantics=("arbitrary", "arbitrary"), vmem_limit_bytes=VMEM_LIMIT),
        name="attention",
    )(qt, kx, vxt)


def _fourier_kernel(u_ref, m1_ref, tr_ref, ts_ref, r2_ref, cc_ref, sc_ref, wbd_ref, y_ref,
                    ma_ref, mb_ref):
    @pl.when(pl.program_id(0) == 0)
    def _():
        w = wbd_ref[...]
        ma_ref[...] = jnp.dot(cc_ref[...], w, precision=lax.Precision.HIGHEST,
                              preferred_element_type=F32).astype(BF16)
        mb_ref[...] = jnp.dot(sc_ref[...], w, precision=lax.Precision.HIGHEST,
                              preferred_element_type=F32).astype(BF16)

    x = u_ref[...].reshape(DFT_N1, DFT_N2, FOURIER_WIDTH)
    xt = jnp.swapaxes(x, 0, 1)

    m1 = m1_ref[...]
    q = []
    for n2 in range(DFT_N2):
        p = jnp.dot(m1, xt[n2], preferred_element_type=F32)
        pr, pi = p[:DFT_N1], p[DFT_N1:]
        tr, ts = tr_ref[n2], ts_ref[n2]
        q.append(jnp.concatenate([pr * tr + pi * ts, pi * tr - pr * ts], axis=0).astype(BF16))
    q = jnp.swapaxes(jnp.stack(q, axis=0), 0, 1)

    r2 = r2_ref[...]
    half = FOURIER_WIDTH // 2
    y = []
    for g in range(DFT_N1 // DFT_K1_GROUP):
        xr, xi = [], []
        for k1 in range(g * DFT_K1_GROUP, (g + 1) * DFT_K1_GROUP):
            qk = jnp.concatenate([q[k1], q[DFT_N1 + k1]], axis=0)
            xk = jnp.dot(r2, qk, preferred_element_type=F32)
            xr.append(xk[:DFT_N2].astype(BF16))
            xi.append(xk[DFT_N2:].astype(BF16))
        xr = jnp.concatenate(xr, axis=0)
        xi = jnp.concatenate(xi, axis=0)
        y.append(jnp.concatenate(
            [jnp.dot(xr[:, lanes], ma_ref[lanes, lanes], preferred_element_type=F32)
             + jnp.dot(xi[:, lanes], mb_ref[lanes, lanes], preferred_element_type=F32)
             for lanes in (slice(0, half), slice(half, FOURIER_WIDTH))], axis=1))
    y = jnp.concatenate(y, axis=0).astype(BF16).reshape(DFT_N1, DFT_N2, FOURIER_WIDTH)
    y_ref[...] = jnp.swapaxes(y, 0, 1).reshape(DFT_N1 * DFT_N2, FOURIER_WIDTH)


def _fourier(u, m1, tr, ts, r2, cc, sc, wbd, batch, seq):
    return pl.pallas_call(
        _fourier_kernel,
        grid=(batch,),
        in_specs=[
            pl.BlockSpec((seq, FOURIER_WIDTH), lambda b: (b, 0)),
            _resident((2 * DFT_N1, DFT_N1)),
            _resident((DFT_N2, DFT_N1, FOURIER_WIDTH)),
            _resident((DFT_N2, DFT_N1, FOURIER_WIDTH)),
            _resident((2 * DFT_N2, 2 * DFT_N2)),
            _resident((FOURIER_WIDTH, FOURIER_WIDTH)),
            _resident((FOURIER_WIDTH, FOURIER_WIDTH)),
            _resident((FOURIER_WIDTH, FOURIER_WIDTH)),
        ],
        out_specs=pl.BlockSpec((seq, FOURIER_WIDTH), lambda b: (b, 0)),
        out_shape=jax.ShapeDtypeStruct((batch * seq, FOURIER_WIDTH), BF16),
        scratch_shapes=[pltpu.VMEM((FOURIER_WIDTH, FOURIER_WIDTH), BF16),
                        pltpu.VMEM((FOURIER_WIDTH, FOURIER_WIDTH), BF16)],
        compiler_params=pltpu.CompilerParams(
            dimension_semantics=("arbitrary",), vmem_limit_bytes=VMEM_LIMIT),
        name="fourier",
    )(u, m1, tr, ts, r2, cc, sc, wbd)


def _out_mlp_kernel(a_ref, f_ref, x_ref, wo_ref, g2_ref, wu_ref, wd_ref, g3_ref, o_ref):
    wo_a = wo_ref[:ATTN_WIDTH, :].astype(BF16)
    wo_f = wo_ref[ATTN_WIDTH:, :].astype(BF16)
    g2 = g2_ref[...]
    g3 = g3_ref[...]

    def mixed(i):
        rows = slice(i * MLP_SUB, (i + 1) * MLP_SUB)
        mix = (jnp.dot(a_ref[rows, :], wo_a, preferred_element_type=F32)
               + jnp.dot(f_ref[rows, :], wo_f, preferred_element_type=F32))
        x1 = x_ref[rows, :] + mix
        return x1, (x1 * _rms_scale(x1) * g2).astype(BF16)

    def mlp(i, x1, h):
        acc = x1
        for c in range(D_FF // FF_CHUNK):
            cs = slice(c * FF_CHUNK, (c + 1) * FF_CHUNK)
            z = jnp.dot(h, wu_ref[:, cs], preferred_element_type=F32)
            z = jnp.square(jnp.maximum(z, 0.0)).astype(BF16)
            acc = acc + jnp.dot(z, wd_ref[cs, :], preferred_element_type=F32)
        o_ref[i * MLP_SUB:(i + 1) * MLP_SUB, :] = acc * _rms_scale(acc) * g3

    n_sub = TOKEN_TILE // MLP_SUB
    nxt = mixed(0)
    for i in range(n_sub):
        cur = nxt
        if i + 1 < n_sub:
            nxt = mixed(i + 1)
        mlp(i, *cur)


def _out_mlp(attn, four, x2, w_out, g2, w_up, w_down, g3):
    n = x2.shape[0]
    tok = lambda i: (i, 0)
    return pl.pallas_call(
        _out_mlp_kernel,
        grid=(n // TOKEN_TILE,),
        in_specs=[
            pl.BlockSpec((TOKEN_TILE, ATTN_WIDTH), tok),
            pl.BlockSpec((TOKEN_TILE, FOURIER_WIDTH), tok),
            pl.BlockSpec((TOKEN_TILE, D_MODEL), tok),
            _resident((D_MODEL, D_MODEL)),
            _resident((1, D_MODEL)),
            _resident((D_MODEL, D_FF)),
            _resident((D_FF, D_MODEL)),
            _resident((1, D_MODEL)),
        ],
        out_specs=pl.BlockSpec((TOKEN_TILE, D_MODEL), tok),
        out_shape=jax.ShapeDtypeStruct((n, D_MODEL), F32),
        compiler_params=pltpu.CompilerParams(
            dimension_semantics=("arbitrary",), vmem_limit_bytes=VMEM_LIMIT),
        name="out_mlp",
    )(attn, four, x2, w_out, g2, w_up, w_down, g3)


@functools.lru_cache(maxsize=None)
def _rope_tables(seq):
    rows = seq // GRID_W
    row = np.repeat(np.arange(rows), GRID_W).astype(np.float64)
    col = np.tile(np.arange(GRID_W), rows).astype(np.float64)
    inv_freq = ROPE_THETA ** (-np.arange(0, AXIS_DIM, 2, dtype=np.float64) / AXIS_DIM)
    row_ang = row[:, None] * inv_freq[None, :]
    col_ang = col[:, None] * inv_freq[None, :]
    ang = np.concatenate([row_ang, row_ang, col_ang, col_ang], axis=-1)
    ang = np.concatenate([ang, ang], axis=-1)
    first_half = (np.arange(V7X_LANES) % AXIS_DIM) < (AXIS_DIM // 2)
    sin = np.sin(ang)
    f32 = lambda a: a.astype(np.float32)
    return (f32(np.cos(ang)), f32(np.where(first_half, -sin, 0.0)),
            f32(np.where(first_half, 0.0, sin)))


@functools.lru_cache(maxsize=None)
def _dft_constants(seq):
    assert seq == DFT_N1 * DFT_N2
    k1 = np.arange(DFT_N1, dtype=np.int64)
    a1 = 2.0 * np.pi * ((k1[:, None] * k1[None, :]) % DFT_N1) / DFT_N1
    m1 = np.concatenate([np.cos(a1), -np.sin(a1)], axis=0)
    n2 = np.arange(DFT_N2, dtype=np.int64)
    at = 2.0 * np.pi * (n2[:, None] * k1[None, :]) / seq
    lanes = np.ones((1, 1, FOURIER_WIDTH))
    tr = np.cos(at)[:, :, None] * lanes
    ts = np.sin(at)[:, :, None] * lanes
    a2 = 2.0 * np.pi * ((n2[:, None] * n2[None, :]) % DFT_N2) / DFT_N2
    c2, s2 = np.cos(a2), np.sin(a2)
    r2 = np.block([[c2, s2], [-s2, c2]])
    c = np.arange(GROUP_DIM, dtype=np.int64)
    angc = 2.0 * np.pi * ((c[:, None] * c[None, :]) % GROUP_DIM) / GROUP_DIM
    scale = 1.0 / np.sqrt(float(seq * GROUP_DIM))
    eye = np.eye(N_GROUPS)
    cc = np.kron(eye, np.cos(angc)) * scale
    sc = np.kron(eye, np.sin(angc)) * scale
    ones_bd = np.kron(np.eye(V7X_LANES // HEAD_DIM), np.ones((HEAD_DIM, HEAD_DIM)))
    f32 = lambda a: a.astype(np.float32)
    return f32(m1), f32(tr), f32(ts), f32(r2), f32(cc), f32(sc), f32(ones_bd)


def kernel(x, mix_norm_g, w_in, q_norm_g, k_norm_g, w_fourier, w_out, mlp_norm_g, w_up,
           w_down, final_norm_g):
    batch, seq, d_model = x.shape
    assert d_model == D_MODEL and seq % TOKEN_TILE == 0 and seq % Q_TILE == 0 and seq % IN_TILE == 0
    x2 = x.reshape(batch * seq, d_model)

    m1_np, tr_np, ts_np, r2_np, cc_np, sc_np, ones_np = _dft_constants(seq)
    m1 = jnp.asarray(m1_np).astype(BF16)
    r2 = jnp.asarray(r2_np).astype(BF16)
    tr, ts, cc, sc = (jnp.asarray(a) for a in (tr_np, ts_np, cc_np, sc_np))
    ones_bd = jnp.asarray(ones_np, dtype=BF16)
    cos_t, sin_a, sin_b = (jnp.asarray(a) for a in _rope_tables(seq))

    row = lambda g: g.reshape(1, -1).astype(F32)
    two_heads = lambda g: jnp.tile(g.astype(F32), V7X_LANES // HEAD_DIM).reshape(1, V7X_LANES)
    wbd = (jnp.eye(N_GROUPS, dtype=F32)[:, None, :, None]
           * w_fourier.astype(F32)[:, :, None, :]).reshape(FOURIER_WIDTH, FOURIER_WIDTH)

    qt, kx, vxt, u = _in_proj(x2, row(mix_norm_g), w_in.astype(F32), ones_bd,
                            two_heads(q_norm_g), two_heads(k_norm_g),
                            cos_t, sin_a, sin_b, batch, seq)
    attn = _attention(qt, kx, vxt, batch, seq)
    four = _fourier(u, m1, tr, ts, r2, cc, sc, wbd, batch, seq)
    out = _out_mlp(attn, four, x2, w_out.astype(F32), row(mlp_norm_g),
                   w_up.astype(BF16), w_down.astype(BF16), row(final_norm_g))
    return out.reshape(batch, seq, d_model)
```

```python
import functools

import numpy as np
import jax
import jax.numpy as jnp
from jax import lax
from jax.experimental import pallas as pl
from jax.experimental.pallas import tpu as pltpu

D_MODEL = 1024
HEAD_DIM = 64
N_Q_HEADS = 8
N_KV_HEADS = 2
ATTN_WIDTH = N_Q_HEADS * HEAD_DIM
KV_WIDTH = N_KV_HEADS * HEAD_DIM
N_GROUPS = 8
GROUP_DIM = 64
FOURIER_WIDTH = N_GROUPS * GROUP_DIM
IN_PROJ_WIDTH = ATTN_WIDTH + 2 * KV_WIDTH + FOURIER_WIDTH
D_FF = 4 * D_MODEL
GRID_W = 64
AXIS_DIM = HEAD_DIM // 2
ROPE_THETA = 10000.0
NORM_EPS = 1e-6
LOG2_E = 1.4426950408889634

V7X_LANES = 128
V7X_VMEM_BYTES = 64 * 1024 * 1024

IN_TILE = 2048
IN_SUB = 256
Q_TILE = 512
FF_CHUNK = 1024
DFT_N1 = 32
DFT_N2 = 64
DFT_K1_GROUP = 4
VMEM_LIMIT = 56 * 1024 * 1024
FUSED_VMEM_LIMIT = 60 * 1024 * 1024

BF16 = jnp.bfloat16
F32 = jnp.float32


def _resident(shape):
    zeros = (0,) * len(shape)
    return pl.BlockSpec(shape, lambda *_: zeros, pipeline_mode=pl.Buffered(1))


def _rms_scale(x):
    return lax.rsqrt(jnp.mean(x * x, axis=-1, keepdims=True) + NORM_EPS)


def _in_proj_kernel(x_ref, g_ref, w_ref, ones_ref, qg_ref, kg_ref, cos_ref, sa_ref,
                    sb_ref, qt_ref, kx_ref, vxt_ref, u_ref):
    g = g_ref[...]
    ones = ones_ref[...]
    qg = qg_ref[...]
    kg = kg_ref[...]
    w = w_ref[...].astype(BF16)

    def project(i):
        x = x_ref[i * IN_SUB:(i + 1) * IN_SUB, :]
        h = x * _rms_scale(x) * g
        return jnp.dot(h.astype(BF16), w, preferred_element_type=F32)

    def finish(i, proj):
        rows = slice(i * IN_SUB, (i + 1) * IN_SUB)
        cos = cos_ref[rows, :]
        sin_a = sa_ref[rows, :]
        sin_b = sb_ref[rows, :]

        def norm_rope(blk, gain, scale):
            ss = jnp.dot((blk * blk).astype(BF16), ones, preferred_element_type=F32)
            y = blk * (lax.rsqrt(ss * (1.0 / HEAD_DIM) + NORM_EPS) * scale) * gain
            return (y * cos + pltpu.roll(y, V7X_LANES - 16, 1) * sin_a
                    + pltpu.roll(y, 16, 1) * sin_b)

        for c in range(ATTN_WIDTH // V7X_LANES):
            sl = slice(c * V7X_LANES, (c + 1) * V7X_LANES)
            qt_ref[0, sl, rows] = norm_rope(
                proj[:, sl], qg, HEAD_DIM ** -0.5 * LOG2_E).T.astype(BF16)

        k = norm_rope(proj[:, ATTN_WIDTH:ATTN_WIDTH + KV_WIDTH], kg, 1.0)
        kr = pltpu.roll(k, HEAD_DIM, 1)
        lo = lax.broadcasted_iota(jnp.int32, k.shape, 1) < HEAD_DIM
        for v, blk in enumerate((jnp.where(lo, k, 0.0), jnp.where(lo, 0.0, kr),
                                 jnp.where(lo, kr, 0.0), jnp.where(lo, 0.0, k))):
            kx_ref[rows, v * V7X_LANES:(v + 1) * V7X_LANES] = blk.astype(BF16)

        vt = proj[:, ATTN_WIDTH + KV_WIDTH:ATTN_WIDTH + 2 * KV_WIDTH].T.astype(BF16)
        top, bot = vt[:HEAD_DIM], vt[HEAD_DIM:]
        one = jnp.ones_like(top)
        for v, (first, second) in enumerate(((top, one), (one, top), (bot, one), (one, bot))):
            vxt_ref[0, v * V7X_LANES:v * V7X_LANES + HEAD_DIM, rows] = first
            vxt_ref[0, v * V7X_LANES + HEAD_DIM:(v + 1) * V7X_LANES, rows] = second

        u_ref[rows, :] = proj[:, ATTN_WIDTH + 2 * KV_WIDTH:].astype(BF16)

    n_sub = IN_TILE // IN_SUB
    nxt = project(0)
    for i in range(n_sub):
        cur = nxt
        if i + 1 < n_sub:
            nxt = project(i + 1)
        finish(i, cur)


def _in_proj(x2, mix_g, w_in, ones_bd, qg, kg, cos_t, sin_a, sin_b, batch, seq):
    n = x2.shape[0]
    tiles_per_seq = seq // IN_TILE
    tok = lambda i: (i, 0)
    pos = lambda i: (i % tiles_per_seq, 0)
    chan_major = lambda i: (i // tiles_per_seq, 0, i % tiles_per_seq)
    return pl.pallas_call(
        _in_proj_kernel,
        grid=(n // IN_TILE,),
        in_specs=[
            pl.BlockSpec((IN_TILE, D_MODEL), tok),
            _resident((1, D_MODEL)),
            _resident((D_MODEL, IN_PROJ_WIDTH)),
            _resident((V7X_LANES, V7X_LANES)),
            _resident((1, V7X_LANES)),
            _resident((1, V7X_LANES)),
            pl.BlockSpec((IN_TILE, V7X_LANES), pos),
            pl.BlockSpec((IN_TILE, V7X_LANES), pos),
            pl.BlockSpec((IN_TILE, V7X_LANES), pos),
        ],
        out_specs=[
            pl.BlockSpec((1, ATTN_WIDTH, IN_TILE), chan_major),
            pl.BlockSpec((IN_TILE, 4 * V7X_LANES), tok),
            pl.BlockSpec((1, 4 * V7X_LANES, IN_TILE), chan_major),
            pl.BlockSpec((IN_TILE, FOURIER_WIDTH), tok),
        ],
        out_shape=[
            jax.ShapeDtypeStruct((batch, ATTN_WIDTH, seq), BF16),
            jax.ShapeDtypeStruct((n, 4 * V7X_LANES), BF16),
            jax.ShapeDtypeStruct((batch, 4 * V7X_LANES, seq), BF16),
            jax.ShapeDtypeStruct((n, FOURIER_WIDTH), BF16),
        ],
        compiler_params=pltpu.CompilerParams(
            dimension_semantics=("arbitrary",), vmem_limit_bytes=VMEM_LIMIT),
        name="in_proj",
    )(x2, mix_g, w_in, ones_bd, qg, kg, cos_t, sin_a, sin_b)


def _fourier_kernel(u_ref, m1_ref, tr_ref, ts_ref, r2_ref, cc_ref, sc_ref, wbd_ref, y_ref,
                    ma_ref, mb_ref):
    @pl.when(pl.program_id(0) == 0)
    def _():
        w = wbd_ref[...]
        ma_ref[...] = jnp.dot(cc_ref[...], w, precision=lax.Precision.HIGHEST,
                              preferred_element_type=F32).astype(BF16)
        mb_ref[...] = jnp.dot(sc_ref[...], w, precision=lax.Precision.HIGHEST,
                              preferred_element_type=F32).astype(BF16)

    x = u_ref[...].reshape(DFT_N1, DFT_N2, FOURIER_WIDTH)
    xt = jnp.swapaxes(x, 0, 1)

    m1 = m1_ref[...]
    q = []
    for n2 in range(DFT_N2):
        p = jnp.dot(m1, xt[n2], preferred_element_type=F32)
        pr, pi = p[:DFT_N1], p[DFT_N1:]
        tr, ts = tr_ref[n2], ts_ref[n2]
        q.append(jnp.concatenate([pr * tr + pi * ts, pi * tr - pr * ts], axis=0).astype(BF16))
    q = jnp.swapaxes(jnp.stack(q, axis=0), 0, 1)

    r2 = r2_ref[...]
    half = FOURIER_WIDTH // 2
    y = []
    for g in range(DFT_N1 // DFT_K1_GROUP):
        xr, xi = [], []
        for k1 in range(g * DFT_K1_GROUP, (g + 1) * DFT_K1_GROUP):
            qk = jnp.concatenate([q[k1], q[DFT_N1 + k1]], axis=0)
            xk = jnp.dot(r2, qk, preferred_element_type=F32)
            xr.append(xk[:DFT_N2].astype(BF16))
            xi.append(xk[DFT_N2:].astype(BF16))
        xr = jnp.concatenate(xr, axis=0)
        xi = jnp.concatenate(xi, axis=0)
        y.append(jnp.concatenate(
            [jnp.dot(xr[:, lanes], ma_ref[lanes, lanes], preferred_element_type=F32)
             + jnp.dot(xi[:, lanes], mb_ref[lanes, lanes], preferred_element_type=F32)
             for lanes in (slice(0, half), slice(half, FOURIER_WIDTH))], axis=1))
    y = jnp.concatenate(y, axis=0).astype(BF16).reshape(DFT_N1, DFT_N2, FOURIER_WIDTH)
    y_ref[...] = jnp.swapaxes(y, 0, 1).reshape(DFT_N1 * DFT_N2, FOURIER_WIDTH)


def _fourier(u, m1, tr, ts, r2, cc, sc, wbd, batch, seq):
    return pl.pallas_call(
        _fourier_kernel,
        grid=(batch,),
        in_specs=[
            pl.BlockSpec((seq, FOURIER_WIDTH), lambda b: (b, 0)),
            _resident((2 * DFT_N1, DFT_N1)),
            _resident((DFT_N2, DFT_N1, FOURIER_WIDTH)),
            _resident((DFT_N2, DFT_N1, FOURIER_WIDTH)),
            _resident((2 * DFT_N2, 2 * DFT_N2)),
            _resident((FOURIER_WIDTH, FOURIER_WIDTH)),
            _resident((FOURIER_WIDTH, FOURIER_WIDTH)),
            _resident((FOURIER_WIDTH, FOURIER_WIDTH)),
        ],
        out_specs=pl.BlockSpec((seq, FOURIER_WIDTH), lambda b: (b, 0)),
        out_shape=jax.ShapeDtypeStruct((batch * seq, FOURIER_WIDTH), BF16),
        scratch_shapes=[pltpu.VMEM((FOURIER_WIDTH, FOURIER_WIDTH), BF16),
                        pltpu.VMEM((FOURIER_WIDTH, FOURIER_WIDTH), BF16)],
        compiler_params=pltpu.CompilerParams(
            dimension_semantics=("arbitrary",), vmem_limit_bytes=VMEM_LIMIT),
        name="fourier",
    )(u, m1, tr, ts, r2, cc, sc, wbd)


def _attn_mlp_kernel(qt_ref, kx_ref, vxt_ref, f_ref, x_ref, wo_ref, g2_ref, wu_ref, wd_ref,
                     g3_ref, o_ref, attn_ref):
    @pl.when(pl.program_id(0) == 0)
    def _():
        attn_ref[...] = jnp.zeros_like(attn_ref)

    top = lax.broadcasted_iota(jnp.int32, (V7X_LANES, Q_TILE), 0) < HEAD_DIM

    def variant(h):
        v = 2 * (h // (N_Q_HEADS // N_KV_HEADS)) + h % 2
        return slice(v * V7X_LANES, (v + 1) * V7X_LANES)

    def scores(h):
        c = h // 2
        qc = qt_ref[0, c * V7X_LANES:(c + 1) * V7X_LANES, :]
        return jnp.dot(kx_ref[:, variant(h)], qc, preferred_element_type=F32)

    mlp = {}

    def mix():
        m = (jnp.dot(attn_ref[...], wo_ref[:ATTN_WIDTH, :], preferred_element_type=F32)
             + jnp.dot(f_ref[...], wo_ref[ATTN_WIDTH:, :], preferred_element_type=F32))
        x1 = x_ref[...] + m
        mlp["acc"] = x1
        mlp["h"] = (x1 * _rms_scale(x1) * g2_ref[...]).astype(BF16)

    def up(c):
        z = jnp.dot(mlp["h"], wu_ref[:, c * FF_CHUNK:(c + 1) * FF_CHUNK],
                    preferred_element_type=F32)
        mlp["z"] = jnp.square(jnp.maximum(z, 0.0)).astype(BF16)

    def down(c):
        mlp["acc"] = mlp["acc"] + jnp.dot(mlp["z"], wd_ref[c * FF_CHUNK:(c + 1) * FF_CHUNK, :],
                                         preferred_element_type=F32)

    units = []
    for c in range(D_FF // FF_CHUNK):
        units += [functools.partial(up, c), functools.partial(down, c)]
    assert len(units) == N_Q_HEADS

    s_next = scores(0)
    mix()
    halves = []
    for h in range(N_Q_HEADS):
        s = s_next
        if h + 1 < N_Q_HEADS:
            s_next = scores(h + 1)
        units[h]()
        m = jnp.max(s, axis=0, keepdims=True)
        e = jnp.exp2(s - m).astype(BF16)
        r = jnp.dot(vxt_ref[0, variant(h), :], e, preferred_element_type=F32)
        halves.append(r / jnp.concatenate([r[HEAD_DIM:], r[:HEAD_DIM]], axis=0))
        if h % 2 == 1:
            c = h // 2
            pair = jnp.where(top, halves[0], halves[1])
            attn_ref[:, c * V7X_LANES:(c + 1) * V7X_LANES] = pair.T.astype(BF16)
            halves = []

    acc = mlp["acc"]
    o_ref[...] = acc * _rms_scale(acc) * g3_ref[...]


def _attn_mlp(qt, kx, vxt, four, x2, w_out, g2, w_up, w_down, g3, batch, seq):
    n = x2.shape[0]
    tiles = n // Q_TILE
    per_seq = seq // Q_TILE
    attn_tile = lambda i: jnp.minimum(i, tiles - 1)
    mlp_tile = lambda i: jnp.maximum(i - 1, 0)
    return pl.pallas_call(
        _attn_mlp_kernel,
        grid=(tiles + 1,),
        in_specs=[
            pl.BlockSpec((1, ATTN_WIDTH, Q_TILE),
                         lambda i: (attn_tile(i) // per_seq, 0, attn_tile(i) % per_seq)),
            pl.BlockSpec((seq, 4 * V7X_LANES), lambda i: (attn_tile(i) // per_seq, 0)),
            pl.BlockSpec((1, 4 * V7X_LANES, seq), lambda i: (attn_tile(i) // per_seq, 0, 0)),
            pl.BlockSpec((Q_TILE, FOURIER_WIDTH), lambda i: (mlp_tile(i), 0)),
            pl.BlockSpec((Q_TILE, D_MODEL), lambda i: (mlp_tile(i), 0)),
            _resident((D_MODEL, D_MODEL)),
            _resident((1, D_MODEL)),
            _resident((D_MODEL, D_FF)),
            _resident((D_FF, D_MODEL)),
            _resident((1, D_MODEL)),
        ],
        out_specs=pl.BlockSpec((Q_TILE, D_MODEL), lambda i: (mlp_tile(i), 0)),
        out_shape=jax.ShapeDtypeStruct((n, D_MODEL), F32),
        scratch_shapes=[pltpu.VMEM((Q_TILE, ATTN_WIDTH), BF16)],
        compiler_params=pltpu.CompilerParams(
            dimension_semantics=("arbitrary",), vmem_limit_bytes=FUSED_VMEM_LIMIT),
        name="attn_mlp",
    )(qt, kx, vxt, four, x2, w_out, g2, w_up, w_down, g3)


@functools.lru_cache(maxsize=None)
def _rope_tables(seq):
    rows = seq // GRID_W
    row = np.repeat(np.arange(rows), GRID_W).astype(np.float64)
    col = np.tile(np.arange(GRID_W), rows).astype(np.float64)
    inv_freq = ROPE_THETA ** (-np.arange(0, AXIS_DIM, 2, dtype=np.float64) / AXIS_DIM)
    row_ang = row[:, None] * inv_freq[None, :]
    col_ang = col[:, None] * inv_freq[None, :]
    ang = np.concatenate([row_ang, row_ang, col_ang, col_ang], axis=-1)
    ang = np.concatenate([ang, ang], axis=-1)
    first_half = (np.arange(V7X_LANES) % AXIS_DIM) < (AXIS_DIM // 2)
    sin = np.sin(ang)
    f32 = lambda a: a.astype(np.float32)
    return (f32(np.cos(ang)), f32(np.where(first_half, -sin, 0.0)),
            f32(np.where(first_half, 0.0, sin)))


@functools.lru_cache(maxsize=None)
def _dft_constants(seq):
    assert seq == DFT_N1 * DFT_N2
    k1 = np.arange(DFT_N1, dtype=np.int64)
    a1 = 2.0 * np.pi * ((k1[:, None] * k1[None, :]) % DFT_N1) / DFT_N1
    m1 = np.concatenate([np.cos(a1), -np.sin(a1)], axis=0)
    n2 = np.arange(DFT_N2, dtype=np.int64)
    at = 2.0 * np.pi * (n2[:, None] * k1[None, :]) / seq
    lanes = np.ones((1, 1, FOURIER_WIDTH))
    tr = np.cos(at)[:, :, None] * lanes
    ts = np.sin(at)[:, :, None] * lanes
    a2 = 2.0 * np.pi * ((n2[:, None] * n2[None, :]) % DFT_N2) / DFT_N2
    c2, s2 = np.cos(a2), np.sin(a2)
    r2 = np.block([[c2, s2], [-s2, c2]])
    c = np.arange(GROUP_DIM, dtype=np.int64)
    angc = 2.0 * np.pi * ((c[:, None] * c[None, :]) % GROUP_DIM) / GROUP_DIM
    scale = 1.0 / np.sqrt(float(seq * GROUP_DIM))
    eye = np.eye(N_GROUPS)
    cc = np.kron(eye, np.cos(angc)) * scale
    sc = np.kron(eye, np.sin(angc)) * scale
    ones_bd = np.kron(np.eye(V7X_LANES // HEAD_DIM), np.ones((HEAD_DIM, HEAD_DIM)))
    f32 = lambda a: a.astype(np.float32)
    return f32(m1), f32(tr), f32(ts), f32(r2), f32(cc), f32(sc), f32(ones_bd)


def kernel(x, mix_norm_g, w_in, q_norm_g, k_norm_g, w_fourier, w_out, mlp_norm_g, w_up,
           w_down, final_norm_g):
    batch, seq, d_model = x.shape
    assert d_model == D_MODEL and seq % Q_TILE == 0 and seq % IN_TILE == 0
    x2 = x.reshape(batch * seq, d_model)

    m1_np, tr_np, ts_np, r2_np, cc_np, sc_np, ones_np = _dft_constants(seq)
    m1 = jnp.asarray(m1_np).astype(BF16)
    r2 = jnp.asarray(r2_np).astype(BF16)
    tr, ts, cc, sc = (jnp.asarray(a) for a in (tr_np, ts_np, cc_np, sc_np))
    ones_bd = jnp.asarray(ones_np, dtype=BF16)
    cos_t, sin_a, sin_b = (jnp.asarray(a) for a in _rope_tables(seq))

    row = lambda g: g.reshape(1, -1).astype(F32)
    two_heads = lambda g: jnp.tile(g.astype(F32), V7X_LANES // HEAD_DIM).reshape(1, V7X_LANES)
    wbd = (jnp.eye(N_GROUPS, dtype=F32)[:, None, :, None]
           * w_fourier.astype(F32)[:, :, None, :]).reshape(FOURIER_WIDTH, FOURIER_WIDTH)

    qt, kx, vxt, u = _in_proj(x2, row(mix_norm_g), w_in.astype(F32), ones_bd,
                            two_heads(q_norm_g), two_heads(k_norm_g),
                            cos_t, sin_a, sin_b, batch, seq)
    four = _fourier(u, m1, tr, ts, r2, cc, sc, wbd, batch, seq)
    out = _attn_mlp(qt, kx, vxt, four, x2, w_out.astype(BF16), row(mlp_norm_g),
                    w_up.astype(BF16), w_down.astype(BF16), row(final_norm_g), batch, seq)
    return out.reshape(batch, seq, d_model)
```

```python
import functools

import numpy as np
import jax
import jax.numpy as jnp
from jax import lax
from jax.experimental import pallas as pl
from jax.experimental.pallas import tpu as pltpu

D_MODEL = 1024
HEAD_DIM = 64
N_Q_HEADS = 8
N_KV_HEADS = 2
ATTN_WIDTH = N_Q_HEADS * HEAD_DIM
KV_WIDTH = N_KV_HEADS * HEAD_DIM
N_GROUPS = 8
GROUP_DIM = 64
FOURIER_WIDTH = N_GROUPS * GROUP_DIM
IN_PROJ_WIDTH = ATTN_WIDTH + 2 * KV_WIDTH + FOURIER_WIDTH
D_FF = 4 * D_MODEL
GRID_W = 64
AXIS_DIM = HEAD_DIM // 2
ROPE_THETA = 10000.0
NORM_EPS = 1e-6
LOG2_E = 1.4426950408889634

V7X_LANES = 128
V7X_VMEM_BYTES = 64 * 1024 * 1024

IN_TILE = 2048
IN_SUB = 256
Q_TILE = 512
FF_CHUNK = 1024
DFT_N1 = 32
DFT_N2 = 64
DFT_K1_GROUP = 4
VMEM_LIMIT = 56 * 1024 * 1024
FUSED_VMEM_LIMIT = 60 * 1024 * 1024

BF16 = jnp.bfloat16
F32 = jnp.float32


def _resident(shape):
    zeros = (0,) * len(shape)
    return pl.BlockSpec(shape, lambda *_: zeros, pipeline_mode=pl.Buffered(1))


def _rms_scale(x):
    return lax.rsqrt(jnp.mean(x * x, axis=-1, keepdims=True) + NORM_EPS)


def _in_proj_kernel(x_ref, g_ref, w_ref, ones_ref, qg_ref, kg_ref, cos_ref, sa_ref,
                    sb_ref, wo_ref, wu_ref, wd_ref,
                    qt_ref, kx_ref, vxt_ref, u_ref, wo16_ref, wu16_ref, wd16_ref):
    wo16_ref[...] = wo_ref[...].astype(BF16)
    wu16_ref[...] = wu_ref[...].astype(BF16)
    wd16_ref[...] = wd_ref[...].astype(BF16)

    g = g_ref[...]
    ones = ones_ref[...]
    qg = qg_ref[...]
    kg = kg_ref[...]
    w = w_ref[...].astype(BF16)

    def project(i):
        x = x_ref[i * IN_SUB:(i + 1) * IN_SUB, :]
        h = x * _rms_scale(x) * g
        return jnp.dot(h.astype(BF16), w, preferred_element_type=F32)

    def finish(i, proj):
        rows = slice(i * IN_SUB, (i + 1) * IN_SUB)
        cos = cos_ref[rows, :]
        sin_a = sa_ref[rows, :]
        sin_b = sb_ref[rows, :]

        def norm_rope(blk, gain, scale):
            ss = jnp.dot((blk * blk).astype(BF16), ones, preferred_element_type=F32)
            y = blk * (lax.rsqrt(ss * (1.0 / HEAD_DIM) + NORM_EPS) * scale) * gain
            return (y * cos + pltpu.roll(y, V7X_LANES - 16, 1) * sin_a
                    + pltpu.roll(y, 16, 1) * sin_b)

        for c in range(ATTN_WIDTH // V7X_LANES):
            sl = slice(c * V7X_LANES, (c + 1) * V7X_LANES)
            qt_ref[0, sl, rows] = norm_rope(
                proj[:, sl], qg, HEAD_DIM ** -0.5 * LOG2_E).T.astype(BF16)

        k = norm_rope(proj[:, ATTN_WIDTH:ATTN_WIDTH + KV_WIDTH], kg, 1.0)
        kr = pltpu.roll(k, HEAD_DIM, 1)
        lo = lax.broadcasted_iota(jnp.int32, k.shape, 1) < HEAD_DIM
        for v, blk in enumerate((jnp.where(lo, k, 0.0), jnp.where(lo, 0.0, kr),
                                 jnp.where(lo, kr, 0.0), jnp.where(lo, 0.0, k))):
            kx_ref[rows, v * V7X_LANES:(v + 1) * V7X_LANES] = blk.astype(BF16)

        vt = proj[:, ATTN_WIDTH + KV_WIDTH:ATTN_WIDTH + 2 * KV_WIDTH].T.astype(BF16)
        one = jnp.ones((HEAD_DIM, vt.shape[1]), BF16)
        for g in range(N_KV_HEADS):
            vxt_ref[0, g * V7X_LANES:g * V7X_LANES + HEAD_DIM, rows] = (
                vt[g * HEAD_DIM:(g + 1) * HEAD_DIM])
            vxt_ref[0, g * V7X_LANES + HEAD_DIM:(g + 1) * V7X_LANES, rows] = one

        u_ref[rows, :] = proj[:, ATTN_WIDTH + 2 * KV_WIDTH:].astype(BF16)

    n_sub = IN_TILE // IN_SUB
    nxt = project(0)
    for i in range(n_sub):
        cur = nxt
        if i + 1 < n_sub:
            nxt = project(i + 1)
        finish(i, cur)


def _in_proj(x2, mix_g, w_in, ones_bd, qg, kg, cos_t, sin_a, sin_b, w_out, w_up, w_down,
             batch, seq):
    n = x2.shape[0]
    steps = n // IN_TILE
    tiles_per_seq = seq // IN_TILE
    tok = lambda i: (i, 0)
    pos = lambda i: (i % tiles_per_seq, 0)
    chan_major = lambda i: (i // tiles_per_seq, 0, i % tiles_per_seq)
    slab = lambda w: pl.BlockSpec((w.shape[0] // steps, w.shape[1]), tok)
    bf16_like = lambda w: jax.ShapeDtypeStruct(w.shape, BF16)
    return pl.pallas_call(
        _in_proj_kernel,
        grid=(steps,),
        in_specs=[
            pl.BlockSpec((IN_TILE, D_MODEL), tok),
            _resident((1, D_MODEL)),
            _resident((D_MODEL, IN_PROJ_WIDTH)),
            _resident((V7X_LANES, V7X_LANES)),
            _resident((1, V7X_LANES)),
            _resident((1, V7X_LANES)),
            pl.BlockSpec((IN_TILE, V7X_LANES), pos),
            pl.BlockSpec((IN_TILE, V7X_LANES), pos),
            pl.BlockSpec((IN_TILE, V7X_LANES), pos),
            slab(w_out), slab(w_up), slab(w_down),
        ],
        out_specs=[
            pl.BlockSpec((1, ATTN_WIDTH, IN_TILE), chan_major),
            pl.BlockSpec((IN_TILE, 4 * V7X_LANES), tok),
            pl.BlockSpec((1, N_KV_HEADS * V7X_LANES, IN_TILE), chan_major),
            pl.BlockSpec((IN_TILE, FOURIER_WIDTH), tok),
            slab(w_out), slab(w_up), slab(w_down),
        ],
        out_shape=[
            jax.ShapeDtypeStruct((batch, ATTN_WIDTH, seq), BF16),
            jax.ShapeDtypeStruct((n, 4 * V7X_LANES), BF16),
            jax.ShapeDtypeStruct((batch, N_KV_HEADS * V7X_LANES, seq), BF16),
            jax.ShapeDtypeStruct((n, FOURIER_WIDTH), BF16),
            bf16_like(w_out), bf16_like(w_up), bf16_like(w_down),
        ],
        compiler_params=pltpu.CompilerParams(
            dimension_semantics=("arbitrary",), vmem_limit_bytes=VMEM_LIMIT),
        name="in_proj",
    )(x2, mix_g, w_in, ones_bd, qg, kg, cos_t, sin_a, sin_b, w_out, w_up, w_down)


def _fourier_kernel(u_ref, m1_ref, tr_ref, ts_ref, r2_ref, cc_ref, sc_ref, wbd_ref, y_ref,
                    ma_ref, mb_ref):
    @pl.when(pl.program_id(0) == 0)
    def _():
        w = wbd_ref[...]
        ma_ref[...] = jnp.dot(cc_ref[...], w, precision=lax.Precision.HIGHEST,
                              preferred_element_type=F32).astype(BF16)
        mb_ref[...] = jnp.dot(sc_ref[...], w, precision=lax.Precision.HIGHEST,
                              preferred_element_type=F32).astype(BF16)

    x = u_ref[...].reshape(DFT_N1, DFT_N2, FOURIER_WIDTH)
    xt = jnp.swapaxes(x, 0, 1)

    m1 = m1_ref[...]
    q = []
    for n2 in range(DFT_N2):
        p = jnp.dot(m1, xt[n2], preferred_element_type=F32)
        pr, pi = p[:DFT_N1], p[DFT_N1:]
        tr, ts = tr_ref[n2], ts_ref[n2]
        q.append(jnp.concatenate([pr * tr + pi * ts, pi * tr - pr * ts], axis=0).astype(BF16))
    q = jnp.swapaxes(jnp.stack(q, axis=0), 0, 1)

    r2 = r2_ref[...]
    half = FOURIER_WIDTH // 2
    y = []
    for g in range(DFT_N1 // DFT_K1_GROUP):
        xr, xi = [], []
        for k1 in range(g * DFT_K1_GROUP, (g + 1) * DFT_K1_GROUP):
            qk = jnp.concatenate([q[k1], q[DFT_N1 + k1]], axis=0)
            xk = jnp.dot(r2, qk, preferred_element_type=F32)
            xr.append(xk[:DFT_N2].astype(BF16))
            xi.append(xk[DFT_N2:].astype(BF16))
        xr = jnp.concatenate(xr, axis=0)
        xi = jnp.concatenate(xi, axis=0)
        y.append(jnp.concatenate(
            [jnp.dot(xr[:, lanes], ma_ref[lanes, lanes], preferred_element_type=F32)
             + jnp.dot(xi[:, lanes], mb_ref[lanes, lanes], preferred_element_type=F32)
             for lanes in (slice(0, half), slice(half, FOURIER_WIDTH))], axis=1))
    y = jnp.concatenate(y, axis=0).astype(BF16).reshape(DFT_N1, DFT_N2, FOURIER_WIDTH)
    y_ref[...] = jnp.swapaxes(y, 0, 1).reshape(DFT_N1 * DFT_N2, FOURIER_WIDTH)


def _fourier(u, m1, tr, ts, r2, cc, sc, wbd, batch, seq):
    return pl.pallas_call(
        _fourier_kernel,
        grid=(batch,),
        in_specs=[
            pl.BlockSpec((seq, FOURIER_WIDTH), lambda b: (b, 0)),
            _resident((2 * DFT_N1, DFT_N1)),
            _resident((DFT_N2, DFT_N1, FOURIER_WIDTH)),
            _resident((DFT_N2, DFT_N1, FOURIER_WIDTH)),
            _resident((2 * DFT_N2, 2 * DFT_N2)),
            _resident((FOURIER_WIDTH, FOURIER_WIDTH)),
            _resident((FOURIER_WIDTH, FOURIER_WIDTH)),
            _resident((FOURIER_WIDTH, FOURIER_WIDTH)),
        ],
        out_specs=pl.BlockSpec((seq, FOURIER_WIDTH), lambda b: (b, 0)),
        out_shape=jax.ShapeDtypeStruct((batch * seq, FOURIER_WIDTH), BF16),
        scratch_shapes=[pltpu.VMEM((FOURIER_WIDTH, FOURIER_WIDTH), BF16),
                        pltpu.VMEM((FOURIER_WIDTH, FOURIER_WIDTH), BF16)],
        compiler_params=pltpu.CompilerParams(
            dimension_semantics=("arbitrary",), vmem_limit_bytes=VMEM_LIMIT),
        name="fourier",
    )(u, m1, tr, ts, r2, cc, sc, wbd)


def _attn_mlp_kernel(qt_ref, kx_ref, vxt_ref, f_ref, x_ref, wo_ref, g2_ref, wu_ref, wd_ref,
                     g3_ref, o_ref, attn_ref):
    @pl.when(pl.program_id(0) == 0)
    def _():
        attn_ref[...] = jnp.zeros_like(attn_ref)

    def variant(h):
        v = 2 * (h // (N_Q_HEADS // N_KV_HEADS)) + h % 2
        return slice(v * V7X_LANES, (v + 1) * V7X_LANES)

    def scores(h):
        c = h // 2
        qc = qt_ref[0, c * V7X_LANES:(c + 1) * V7X_LANES, :]
        return jnp.dot(kx_ref[:, variant(h)], qc, preferred_element_type=F32)

    mlp = {}

    def mix():
        m = (jnp.dot(attn_ref[...], wo_ref[:ATTN_WIDTH, :], preferred_element_type=F32)
             + jnp.dot(f_ref[...], wo_ref[ATTN_WIDTH:, :], preferred_element_type=F32))
        x1 = x_ref[...] + m
        mlp["acc"] = x1
        mlp["h"] = (x1 * _rms_scale(x1) * g2_ref[...]).astype(BF16)

    def up(c):
        z = jnp.dot(mlp["h"], wu_ref[:, c * FF_CHUNK:(c + 1) * FF_CHUNK],
                    preferred_element_type=F32)
        mlp["z"] = jnp.square(jnp.maximum(z, 0.0)).astype(BF16)

    def down(c):
        mlp["acc"] = mlp["acc"] + jnp.dot(mlp["z"], wd_ref[c * FF_CHUNK:(c + 1) * FF_CHUNK, :],
                                         preferred_element_type=F32)

    units = []
    for c in range(D_FF // FF_CHUNK):
        units += [functools.partial(up, c), functools.partial(down, c)]
    assert len(units) == N_Q_HEADS

    s_next = scores(0)
    mix()
    halves = []
    for h in range(N_Q_HEADS):
        s = s_next
        if h + 1 < N_Q_HEADS:
            s_next = scores(h + 1)
        units[h]()
        m = jnp.max(s, axis=0, keepdims=True)
        e = jnp.exp2(s - m).astype(BF16)
        g = h // (N_Q_HEADS // N_KV_HEADS)
        r = jnp.dot(vxt_ref[0, g * V7X_LANES:(g + 1) * V7X_LANES, :], e,
                    preferred_element_type=F32)
        halves.append(r[:HEAD_DIM] / r[HEAD_DIM:])
        if h % 2 == 1:
            c = h // 2
            pair = jnp.concatenate(halves, axis=0)
            attn_ref[:, c * V7X_LANES:(c + 1) * V7X_LANES] = pair.T.astype(BF16)
            halves = []

    acc = mlp["acc"]
    o_ref[...] = acc * _rms_scale(acc) * g3_ref[...]


def _attn_mlp(qt, kx, vxt, four, x2, w_out, g2, w_up, w_down, g3, batch, seq):
    n = x2.shape[0]
    tiles = n // Q_TILE
    per_seq = seq // Q_TILE
    attn_tile = lambda i: jnp.minimum(i, tiles - 1)
    mlp_tile = lambda i: jnp.maximum(i - 1, 0)
    return pl.pallas_call(
        _attn_mlp_kernel,
        grid=(tiles + 1,),
        in_specs=[
            pl.BlockSpec((1, ATTN_WIDTH, Q_TILE),
                         lambda i: (attn_tile(i) // per_seq, 0, attn_tile(i) % per_seq)),
            pl.BlockSpec((seq, 4 * V7X_LANES), lambda i: (attn_tile(i) // per_seq, 0)),
            pl.BlockSpec((1, N_KV_HEADS * V7X_LANES, seq),
                         lambda i: (attn_tile(i) // per_seq, 0, 0)),
            pl.BlockSpec((Q_TILE, FOURIER_WIDTH), lambda i: (mlp_tile(i), 0)),
            pl.BlockSpec((Q_TILE, D_MODEL), lambda i: (mlp_tile(i), 0)),
            _resident((D_MODEL, D_MODEL)),
            _resident((1, D_MODEL)),
            _resident((D_MODEL, D_FF)),
            _resident((D_FF, D_MODEL)),
            _resident((1, D_MODEL)),
        ],
        out_specs=pl.BlockSpec((Q_TILE, D_MODEL), lambda i: (mlp_tile(i), 0)),
        out_shape=jax.ShapeDtypeStruct((n, D_MODEL), F32),
        scratch_shapes=[pltpu.VMEM((Q_TILE, ATTN_WIDTH), BF16)],
        compiler_params=pltpu.CompilerParams(
            dimension_semantics=("arbitrary",), vmem_limit_bytes=FUSED_VMEM_LIMIT),
        name="attn_mlp",
    )(qt, kx, vxt, four, x2, w_out, g2, w_up, w_down, g3)


@functools.lru_cache(maxsize=None)
def _rope_tables(seq):
    rows = seq // GRID_W
    row = np.repeat(np.arange(rows), GRID_W).astype(np.float64)
    col = np.tile(np.arange(GRID_W), rows).astype(np.float64)
    inv_freq = ROPE_THETA ** (-np.arange(0, AXIS_DIM, 2, dtype=np.float64) / AXIS_DIM)
    row_ang = row[:, None] * inv_freq[None, :]
    col_ang = col[:, None] * inv_freq[None, :]
    ang = np.concatenate([row_ang, row_ang, col_ang, col_ang], axis=-1)
    ang = np.concatenate([ang, ang], axis=-1)
    first_half = (np.arange(V7X_LANES) % AXIS_DIM) < (AXIS_DIM // 2)
    sin = np.sin(ang)
    f32 = lambda a: a.astype(np.float32)
    return (f32(np.cos(ang)), f32(np.where(first_half, -sin, 0.0)),
            f32(np.where(first_half, 0.0, sin)))


@functools.lru_cache(maxsize=None)
def _dft_constants(seq):
    assert seq == DFT_N1 * DFT_N2
    k1 = np.arange(DFT_N1, dtype=np.int64)
    a1 = 2.0 * np.pi * ((k1[:, None] * k1[None, :]) % DFT_N1) / DFT_N1
    m1 = np.concatenate([np.cos(a1), -np.sin(a1)], axis=0)
    n2 = np.arange(DFT_N2, dtype=np.int64)
    at = 2.0 * np.pi * (n2[:, None] * k1[None, :]) / seq
    lanes = np.ones((1, 1, FOURIER_WIDTH))
    tr = np.cos(at)[:, :, None] * lanes
    ts = np.sin(at)[:, :, None] * lanes
    a2 = 2.0 * np.pi * ((n2[:, None] * n2[None, :]) % DFT_N2) / DFT_N2
    c2, s2 = np.cos(a2), np.sin(a2)
    r2 = np.block([[c2, s2], [-s2, c2]])
    c = np.arange(GROUP_DIM, dtype=np.int64)
    angc = 2.0 * np.pi * ((c[:, None] * c[None, :]) % GROUP_DIM) / GROUP_DIM
    scale = 1.0 / np.sqrt(float(seq * GROUP_DIM))
    eye = np.eye(N_GROUPS)
    cc = np.kron(eye, np.cos(angc)) * scale
    sc = np.kron(eye, np.sin(angc)) * scale
    ones_bd = np.kron(np.eye(V7X_LANES // HEAD_DIM), np.ones((HEAD_DIM, HEAD_DIM)))
    f32 = lambda a: a.astype(np.float32)
    return f32(m1), f32(tr), f32(ts), f32(r2), f32(cc), f32(sc), f32(ones_bd)


def kernel(x, mix_norm_g, w_in, q_norm_g, k_norm_g, w_fourier, w_out, mlp_norm_g, w_up,
           w_down, final_norm_g):
    batch, seq, d_model = x.shape
    assert d_model == D_MODEL and seq % Q_TILE == 0 and seq % IN_TILE == 0
    x2 = x.reshape(batch * seq, d_model)

    m1_np, tr_np, ts_np, r2_np, cc_np, sc_np, ones_np = _dft_constants(seq)
    m1 = jnp.asarray(m1_np).astype(BF16)
    r2 = jnp.asarray(r2_np).astype(BF16)
    tr, ts, cc, sc = (jnp.asarray(a) for a in (tr_np, ts_np, cc_np, sc_np))
    ones_bd = jnp.asarray(ones_np, dtype=BF16)
    cos_t, sin_a, sin_b = (jnp.asarray(a) for a in _rope_tables(seq))

    row = lambda g: g.reshape(1, -1).astype(F32)
    two_heads = lambda g: jnp.tile(g.astype(F32), V7X_LANES // HEAD_DIM).reshape(1, V7X_LANES)
    wbd = (jnp.eye(N_GROUPS, dtype=F32)[:, None, :, None]
           * w_fourier.astype(F32)[:, :, None, :]).reshape(FOURIER_WIDTH, FOURIER_WIDTH)

    qt, kx, vxt, u, w_out16, w_up16, w_down16 = _in_proj(
        x2, row(mix_norm_g), w_in.astype(F32), ones_bd, two_heads(q_norm_g),
        two_heads(k_norm_g), cos_t, sin_a, sin_b, w_out.astype(F32), w_up.astype(F32),
        w_down.astype(F32), batch, seq)
    four = _fourier(u, m1, tr, ts, r2, cc, sc, wbd, batch, seq)
    out = _attn_mlp(qt, kx, vxt, four, x2, w_out16, row(mlp_norm_g), w_up16, w_down16,
                    row(final_norm_g), batch, seq)
    return out.reshape(batch, seq, d_model)
```

```python
import functools

import numpy as np
import jax
import jax.numpy as jnp
from jax import lax
from jax.experimental import pallas as pl
from jax.experimental.pallas import tpu as pltpu

D_MODEL = 1024
HEAD_DIM = 64
N_Q_HEADS = 8
N_KV_HEADS = 2
ATTN_WIDTH = N_Q_HEADS * HEAD_DIM
KV_WIDTH = N_KV_HEADS * HEAD_DIM
N_GROUPS = 8
GROUP_DIM = 64
FOURIER_WIDTH = N_GROUPS * GROUP_DIM
IN_PROJ_WIDTH = ATTN_WIDTH + 2 * KV_WIDTH + FOURIER_WIDTH
D_FF = 4 * D_MODEL
GRID_W = 64
AXIS_DIM = HEAD_DIM // 2
ROPE_THETA = 10000.0
NORM_EPS = 1e-6
LOG2_E = 1.4426950408889634

V7X_LANES = 128
V7X_VMEM_BYTES = 64 * 1024 * 1024

IN_TILE = 2048
IN_SUB = 256
Q_TILE = 512
FF_CHUNK = 1024
DFT_N1 = 32
DFT_N2 = 64
DFT_K1_GROUP = 4
VMEM_LIMIT = 56 * 1024 * 1024
FUSED_VMEM_LIMIT = 60 * 1024 * 1024

BF16 = jnp.bfloat16
F32 = jnp.float32


def _resident(shape):
    zeros = (0,) * len(shape)
    return pl.BlockSpec(shape, lambda *_: zeros, pipeline_mode=pl.Buffered(1))


def _rms_scale(x):
    return lax.rsqrt(jnp.mean(x * x, axis=-1, keepdims=True) + NORM_EPS)


def _in_proj_kernel(x_ref, g_ref, w_ref, ones_ref, qg_ref, kg_ref, cos_ref, sa_ref,
                    sb_ref, wo_ref, wu_ref, wd_ref,
                    qt_ref, k_ref, vxt_ref, u_ref, wo16_ref, wu16_ref, wd16_ref):
    wo16_ref[...] = wo_ref[...].astype(BF16)
    wu16_ref[...] = wu_ref[...].astype(BF16)
    wd16_ref[...] = wd_ref[...].astype(BF16)

    g = g_ref[...]
    ones = ones_ref[...]
    qg = qg_ref[...]
    kg = kg_ref[...]
    w = w_ref[...].astype(BF16)

    def project(i):
        x = x_ref[i * IN_SUB:(i + 1) * IN_SUB, :]
        h = x * _rms_scale(x) * g
        return jnp.dot(h.astype(BF16), w, preferred_element_type=F32)

    def finish(i, proj):
        rows = slice(i * IN_SUB, (i + 1) * IN_SUB)
        cos = cos_ref[rows, :]
        sin_a = sa_ref[rows, :]
        sin_b = sb_ref[rows, :]

        def norm_rope(blk, gain, scale):
            ss = jnp.dot((blk * blk).astype(BF16), ones, preferred_element_type=F32)
            y = blk * (lax.rsqrt(ss * (1.0 / HEAD_DIM) + NORM_EPS) * scale) * gain
            return (y * cos + pltpu.roll(y, V7X_LANES - 16, 1) * sin_a
                    + pltpu.roll(y, 16, 1) * sin_b)

        for c in range(ATTN_WIDTH // V7X_LANES):
            sl = slice(c * V7X_LANES, (c + 1) * V7X_LANES)
            qt_ref[0, sl, rows] = norm_rope(
                proj[:, sl], qg, HEAD_DIM ** -0.5 * LOG2_E).T.astype(BF16)

        k = norm_rope(proj[:, ATTN_WIDTH:ATTN_WIDTH + KV_WIDTH], kg, 1.0)
        k_ref[rows, :] = k.astype(BF16)

        vt = proj[:, ATTN_WIDTH + KV_WIDTH:ATTN_WIDTH + 2 * KV_WIDTH].T.astype(BF16)
        one = jnp.ones((HEAD_DIM, vt.shape[1]), BF16)
        for g in range(N_KV_HEADS):
            vxt_ref[0, g * V7X_LANES:g * V7X_LANES + HEAD_DIM, rows] = (
                vt[g * HEAD_DIM:(g + 1) * HEAD_DIM])
            vxt_ref[0, g * V7X_LANES + HEAD_DIM:(g + 1) * V7X_LANES, rows] = one

        u_ref[rows, :] = proj[:, ATTN_WIDTH + 2 * KV_WIDTH:].astype(BF16)

    n_sub = IN_TILE // IN_SUB
    nxt = project(0)
    for i in range(n_sub):
        cur = nxt
        if i + 1 < n_sub:
            nxt = project(i + 1)
        finish(i, cur)


def _in_proj(x2, mix_g, w_in, ones_bd, qg, kg, cos_t, sin_a, sin_b, w_out, w_up, w_down,
             batch, seq):
    n = x2.shape[0]
    steps = n // IN_TILE
    tiles_per_seq = seq // IN_TILE
    tok = lambda i: (i, 0)
    pos = lambda i: (i % tiles_per_seq, 0)
    chan_major = lambda i: (i // tiles_per_seq, 0, i % tiles_per_seq)
    slab = lambda w: pl.BlockSpec((w.shape[0] // steps, w.shape[1]), tok)
    bf16_like = lambda w: jax.ShapeDtypeStruct(w.shape, BF16)
    return pl.pallas_call(
        _in_proj_kernel,
        grid=(steps,),
        in_specs=[
            pl.BlockSpec((IN_TILE, D_MODEL), tok),
            _resident((1, D_MODEL)),
            _resident((D_MODEL, IN_PROJ_WIDTH)),
            _resident((V7X_LANES, V7X_LANES)),
            _resident((1, V7X_LANES)),
            _resident((1, V7X_LANES)),
            pl.BlockSpec((IN_TILE, V7X_LANES), pos),
            pl.BlockSpec((IN_TILE, V7X_LANES), pos),
            pl.BlockSpec((IN_TILE, V7X_LANES), pos),
            slab(w_out), slab(w_up), slab(w_down),
        ],
        out_specs=[
            pl.BlockSpec((1, ATTN_WIDTH, IN_TILE), chan_major),
            pl.BlockSpec((IN_TILE, KV_WIDTH), tok),
            pl.BlockSpec((1, N_KV_HEADS * V7X_LANES, IN_TILE), chan_major),
            pl.BlockSpec((IN_TILE, FOURIER_WIDTH), tok),
            slab(w_out), slab(w_up), slab(w_down),
        ],
        out_shape=[
            jax.ShapeDtypeStruct((batch, ATTN_WIDTH, seq), BF16),
            jax.ShapeDtypeStruct((n, KV_WIDTH), BF16),
            jax.ShapeDtypeStruct((batch, N_KV_HEADS * V7X_LANES, seq), BF16),
            jax.ShapeDtypeStruct((n, FOURIER_WIDTH), BF16),
            bf16_like(w_out), bf16_like(w_up), bf16_like(w_down),
        ],
        compiler_params=pltpu.CompilerParams(
            dimension_semantics=("arbitrary",), vmem_limit_bytes=VMEM_LIMIT),
        name="in_proj",
    )(x2, mix_g, w_in, ones_bd, qg, kg, cos_t, sin_a, sin_b, w_out, w_up, w_down)


def _fourier_kernel(u_ref, m1_ref, g2_ref, cc_ref, sc_ref, wbd_ref, y_ref, ma_ref, mb_ref):
    @pl.when(pl.program_id(0) == 0)
    def _():
        w = wbd_ref[...]
        ma_ref[...] = jnp.dot(cc_ref[...], w, precision=lax.Precision.HIGHEST,
                              preferred_element_type=F32).astype(BF16)
        mb_ref[...] = jnp.dot(sc_ref[...], w, precision=lax.Precision.HIGHEST,
                              preferred_element_type=F32).astype(BF16)

    x = u_ref[...].reshape(DFT_N1, DFT_N2, FOURIER_WIDTH)
    xt = jnp.swapaxes(x, 0, 1)

    m1 = m1_ref[...]
    q = []
    for n2 in range(DFT_N2):
        q.append(jnp.dot(m1, xt[n2], preferred_element_type=F32).astype(BF16))
    q = jnp.swapaxes(jnp.stack(q, axis=0), 0, 1)

    half = FOURIER_WIDTH // 2
    y = []
    for g in range(DFT_N1 // DFT_K1_GROUP):
        xr, xi = [], []
        for k1 in range(g * DFT_K1_GROUP, (g + 1) * DFT_K1_GROUP):
            qk = jnp.concatenate([q[k1], q[DFT_N1 + k1]], axis=0)
            xk = jnp.dot(g2_ref[k1], qk, preferred_element_type=F32)
            xr.append(xk[:DFT_N2].astype(BF16))
            xi.append(xk[DFT_N2:].astype(BF16))
        xr = jnp.concatenate(xr, axis=0)
        xi = jnp.concatenate(xi, axis=0)
        y.append(jnp.concatenate(
            [jnp.dot(xr[:, lanes], ma_ref[lanes, lanes], preferred_element_type=F32)
             + jnp.dot(xi[:, lanes], mb_ref[lanes, lanes], preferred_element_type=F32)
             for lanes in (slice(0, half), slice(half, FOURIER_WIDTH))], axis=1))
    y = jnp.concatenate(y, axis=0).astype(BF16).reshape(DFT_N1, DFT_N2, FOURIER_WIDTH)
    y_ref[...] = jnp.swapaxes(y, 0, 1).reshape(DFT_N1 * DFT_N2, FOURIER_WIDTH)


def _fourier(u, m1, g2, cc, sc, wbd, batch, seq):
    return pl.pallas_call(
        _fourier_kernel,
        grid=(batch,),
        in_specs=[
            pl.BlockSpec((seq, FOURIER_WIDTH), lambda b: (b, 0)),
            _resident((2 * DFT_N1, DFT_N1)),
            _resident((DFT_N1, 2 * DFT_N2, 2 * DFT_N2)),
            _resident((FOURIER_WIDTH, FOURIER_WIDTH)),
            _resident((FOURIER_WIDTH, FOURIER_WIDTH)),
            _resident((FOURIER_WIDTH, FOURIER_WIDTH)),
        ],
        out_specs=pl.BlockSpec((seq, FOURIER_WIDTH), lambda b: (b, 0)),
        out_shape=jax.ShapeDtypeStruct((batch * seq, FOURIER_WIDTH), BF16),
        scratch_shapes=[pltpu.VMEM((FOURIER_WIDTH, FOURIER_WIDTH), BF16),
                        pltpu.VMEM((FOURIER_WIDTH, FOURIER_WIDTH), BF16)],
        compiler_params=pltpu.CompilerParams(
            dimension_semantics=("arbitrary",), vmem_limit_bytes=VMEM_LIMIT),
        name="fourier",
    )(u, m1, g2, cc, sc, wbd)


def _attn_mlp_kernel(qt_ref, k_ref, vxt_ref, f_ref, x_ref, wo_ref, g2_ref, wu_ref, wd_ref,
                     g3_ref, o_ref, attn_ref):
    @pl.when(pl.program_id(0) == 0)
    def _():
        attn_ref[...] = jnp.zeros_like(attn_ref)

    def scores(h):
        q = qt_ref[0, h * HEAD_DIM:(h + 1) * HEAD_DIM, :]
        zero = jnp.zeros_like(q)
        g = h // (N_Q_HEADS // N_KV_HEADS)
        q_pad = jnp.concatenate([zero] * g + [q] + [zero] * (N_KV_HEADS - 1 - g), axis=0)
        return jnp.dot(k_ref[...], q_pad, preferred_element_type=F32)

    mlp = {}

    def mix():
        m = (jnp.dot(attn_ref[...], wo_ref[:ATTN_WIDTH, :], preferred_element_type=F32)
             + jnp.dot(f_ref[...], wo_ref[ATTN_WIDTH:, :], preferred_element_type=F32))
        x1 = x_ref[...] + m
        mlp["acc"] = x1
        mlp["h"] = (x1 * _rms_scale(x1) * g2_ref[...]).astype(BF16)

    def up(c):
        z = jnp.dot(mlp["h"], wu_ref[:, c * FF_CHUNK:(c + 1) * FF_CHUNK],
                    preferred_element_type=F32)
        mlp["z"] = jnp.square(jnp.maximum(z, 0.0)).astype(BF16)

    def down(c):
        mlp["acc"] = mlp["acc"] + jnp.dot(mlp["z"], wd_ref[c * FF_CHUNK:(c + 1) * FF_CHUNK, :],
                                         preferred_element_type=F32)

    units = []
    for c in range(D_FF // FF_CHUNK):
        units += [functools.partial(up, c), functools.partial(down, c)]
    assert len(units) == N_Q_HEADS

    s_next = scores(0)
    mix()
    halves = []
    for h in range(N_Q_HEADS):
        s = s_next
        if h + 1 < N_Q_HEADS:
            s_next = scores(h + 1)
        units[h]()
        m = jnp.max(s, axis=0, keepdims=True)
        e = jnp.exp2(s - m).astype(BF16)
        g = h // (N_Q_HEADS // N_KV_HEADS)
        r = jnp.dot(vxt_ref[0, g * V7X_LANES:(g + 1) * V7X_LANES, :], e,
                    preferred_element_type=F32)
        halves.append(r[:HEAD_DIM] / r[HEAD_DIM:])
        if h % 2 == 1:
            c = h // 2
            pair = jnp.concatenate(halves, axis=0)
            attn_ref[:, c * V7X_LANES:(c + 1) * V7X_LANES] = pair.T.astype(BF16)
            halves = []

    acc = mlp["acc"]
    o_ref[...] = acc * _rms_scale(acc) * g3_ref[...]


def _attn_mlp(qt, kx, vxt, four, x2, w_out, g2, w_up, w_down, g3, batch, seq):
    n = x2.shape[0]
    tiles = n // Q_TILE
    per_seq = seq // Q_TILE
    attn_tile = lambda i: jnp.minimum(i, tiles - 1)
    mlp_tile = lambda i: jnp.maximum(i - 1, 0)
    return pl.pallas_call(
        _attn_mlp_kernel,
        grid=(tiles + 1,),
        in_specs=[
            pl.BlockSpec((1, ATTN_WIDTH, Q_TILE),
                         lambda i: (attn_tile(i) // per_seq, 0, attn_tile(i) % per_seq)),
            pl.BlockSpec((seq, KV_WIDTH), lambda i: (attn_tile(i) // per_seq, 0)),
            pl.BlockSpec((1, N_KV_HEADS * V7X_LANES, seq),
                         lambda i: (attn_tile(i) // per_seq, 0, 0)),
            pl.BlockSpec((Q_TILE, FOURIER_WIDTH), lambda i: (mlp_tile(i), 0)),
            pl.BlockSpec((Q_TILE, D_MODEL), lambda i: (mlp_tile(i), 0)),
            _resident((D_MODEL, D_MODEL)),
            _resident((1, D_MODEL)),
            _resident((D_MODEL, D_FF)),
            _resident((D_FF, D_MODEL)),
            _resident((1, D_MODEL)),
        ],
        out_specs=pl.BlockSpec((Q_TILE, D_MODEL), lambda i: (mlp_tile(i), 0)),
        out_shape=jax.ShapeDtypeStruct((n, D_MODEL), F32),
        scratch_shapes=[pltpu.VMEM((Q_TILE, ATTN_WIDTH), BF16)],
        compiler_params=pltpu.CompilerParams(
            dimension_semantics=("arbitrary",), vmem_limit_bytes=FUSED_VMEM_LIMIT),
        name="attn_mlp",
    )(qt, kx, vxt, four, x2, w_out, g2, w_up, w_down, g3)


@functools.lru_cache(maxsize=None)
def _rope_tables(seq):
    rows = seq // GRID_W
    row = np.repeat(np.arange(rows), GRID_W).astype(np.float64)
    col = np.tile(np.arange(GRID_W), rows).astype(np.float64)
    inv_freq = ROPE_THETA ** (-np.arange(0, AXIS_DIM, 2, dtype=np.float64) / AXIS_DIM)
    row_ang = row[:, None] * inv_freq[None, :]
    col_ang = col[:, None] * inv_freq[None, :]
    ang = np.concatenate([row_ang, row_ang, col_ang, col_ang], axis=-1)
    ang = np.concatenate([ang, ang], axis=-1)
    first_half = (np.arange(V7X_LANES) % AXIS_DIM) < (AXIS_DIM // 2)
    sin = np.sin(ang)
    f32 = lambda a: a.astype(np.float32)
    return (f32(np.cos(ang)), f32(np.where(first_half, -sin, 0.0)),
            f32(np.where(first_half, 0.0, sin)))


@functools.lru_cache(maxsize=None)
def _dft_constants(seq):
    assert seq == DFT_N1 * DFT_N2
    k1 = np.arange(DFT_N1, dtype=np.int64)
    a1 = 2.0 * np.pi * ((k1[:, None] * k1[None, :]) % DFT_N1) / DFT_N1
    m1 = np.concatenate([np.cos(a1), -np.sin(a1)], axis=0)
    n2 = np.arange(DFT_N2, dtype=np.int64)
    a2 = 2.0 * np.pi * ((n2[:, None] * n2[None, :]) % DFT_N2) / DFT_N2
    c2, s2 = np.cos(a2), np.sin(a2)
    r2 = np.block([[c2, s2], [-s2, c2]])
    at = 2.0 * np.pi * (k1[:, None] * n2[None, :]) / seq
    g2 = np.stack([r2 @ np.block([[np.diag(np.cos(a)), np.diag(np.sin(a))],
                                  [-np.diag(np.sin(a)), np.diag(np.cos(a))]]) for a in at])
    c = np.arange(GROUP_DIM, dtype=np.int64)
    angc = 2.0 * np.pi * ((c[:, None] * c[None, :]) % GROUP_DIM) / GROUP_DIM
    scale = 1.0 / np.sqrt(float(seq * GROUP_DIM))
    eye = np.eye(N_GROUPS)
    cc = np.kron(eye, np.cos(angc)) * scale
    sc = np.kron(eye, np.sin(angc)) * scale
    ones_bd = np.kron(np.eye(V7X_LANES // HEAD_DIM), np.ones((HEAD_DIM, HEAD_DIM)))
    f32 = lambda a: a.astype(np.float32)
    return f32(m1), f32(g2), f32(cc), f32(sc), f32(ones_bd)


def kernel(x, mix_norm_g, w_in, q_norm_g, k_norm_g, w_fourier, w_out, mlp_norm_g, w_up,
           w_down, final_norm_g):
    batch, seq, d_model = x.shape
    assert d_model == D_MODEL and seq % Q_TILE == 0 and seq % IN_TILE == 0
    x2 = x.reshape(batch * seq, d_model)

    m1_np, g2_np, cc_np, sc_np, ones_np = _dft_constants(seq)
    m1 = jnp.asarray(m1_np).astype(BF16)
    g2 = jnp.asarray(g2_np).astype(BF16)
    cc, sc = jnp.asarray(cc_np), jnp.asarray(sc_np)
    ones_bd = jnp.asarray(ones_np, dtype=BF16)
    cos_t, sin_a, sin_b = (jnp.asarray(a) for a in _rope_tables(seq))

    row = lambda g: g.reshape(1, -1).astype(F32)
    two_heads = lambda g: jnp.tile(g.astype(F32), V7X_LANES // HEAD_DIM).reshape(1, V7X_LANES)
    wbd = (jnp.eye(N_GROUPS, dtype=F32)[:, None, :, None]
           * w_fourier.astype(F32)[:, :, None, :]).reshape(FOURIER_WIDTH, FOURIER_WIDTH)

    qt, kx, vxt, u, w_out16, w_up16, w_down16 = _in_proj(
        x2, row(mix_norm_g), w_in.astype(F32), ones_bd, two_heads(q_norm_g),
        two_heads(k_norm_g), cos_t, sin_a, sin_b, w_out.astype(F32), w_up.astype(F32),
        w_down.astype(F32), batch, seq)
    four = _fourier(u, m1, g2, cc, sc, wbd, batch, seq)
    out = _attn_mlp(qt, kx, vxt, four, x2, w_out16, row(mlp_norm_g), w_up16, w_down16,
                    row(final_norm_g), batch, seq)
    return out.reshape(batch, seq, d_model)
```

```python
import functools

import numpy as np
import jax
import jax.numpy as jnp
from jax import lax
from jax.experimental import pallas as pl
from jax.experimental.pallas import tpu as pltpu

D_MODEL = 1024
HEAD_DIM = 64
N_Q_HEADS = 8
N_KV_HEADS = 2
ATTN_WIDTH = N_Q_HEADS * HEAD_DIM
KV_WIDTH = N_KV_HEADS * HEAD_DIM
N_GROUPS = 8
GROUP_DIM = 64
FOURIER_WIDTH = N_GROUPS * GROUP_DIM
IN_PROJ_WIDTH = ATTN_WIDTH + 2 * KV_WIDTH + FOURIER_WIDTH
D_FF = 4 * D_MODEL
GRID_W = 64
AXIS_DIM = HEAD_DIM // 2
ROPE_THETA = 10000.0
NORM_EPS = 1e-6
LOG2_E = 1.4426950408889634

V7X_LANES = 128
V7X_VMEM_BYTES = 64 * 1024 * 1024

IN_TILE = 2048
IN_SUB = 256
Q_TILE = 512
FF_CHUNK = 1024
DFT_N1 = 32
DFT_N2 = 64
DFT_K1_GROUP = 8
VMEM_LIMIT = 56 * 1024 * 1024
FUSED_VMEM_LIMIT = 60 * 1024 * 1024

BF16 = jnp.bfloat16
F32 = jnp.float32


def _resident(shape):
    zeros = (0,) * len(shape)
    return pl.BlockSpec(shape, lambda *_: zeros, pipeline_mode=pl.Buffered(1))


def _rms_scale(x):
    return lax.rsqrt(jnp.mean(x * x, axis=-1, keepdims=True) + NORM_EPS)


def _in_proj_kernel(x_ref, g_ref, w_ref, ones_ref, qg_ref, kg_ref, cos_ref, sa_ref,
                    sb_ref, wo_ref, wu_ref, wd_ref,
                    qt_ref, k_ref, vxt_ref, u_ref, wo16_ref, wu16_ref, wd16_ref):
    wo16_ref[...] = wo_ref[...].astype(BF16)
    wu16_ref[...] = wu_ref[...].astype(BF16)
    wd16_ref[...] = wd_ref[...].astype(BF16)

    g = g_ref[...]
    ones = ones_ref[...]
    qg = qg_ref[...]
    kg = kg_ref[...]
    w = w_ref[...].astype(BF16)

    def project(i):
        x = x_ref[i * IN_SUB:(i + 1) * IN_SUB, :]
        h = x * _rms_scale(x) * g
        return jnp.dot(h.astype(BF16), w, preferred_element_type=F32)

    def finish(i, proj):
        rows = slice(i * IN_SUB, (i + 1) * IN_SUB)
        cos = cos_ref[rows, :]
        sin_a = sa_ref[rows, :]
        sin_b = sb_ref[rows, :]

        def norm_rope(blk, gain, scale):
            ss = jnp.dot((blk * blk).astype(BF16), ones, preferred_element_type=F32)
            y = blk * (lax.rsqrt(ss * (1.0 / HEAD_DIM) + NORM_EPS) * scale) * gain
            return (y * cos + pltpu.roll(y, V7X_LANES - 16, 1) * sin_a
                    + pltpu.roll(y, 16, 1) * sin_b)

        for c in range(ATTN_WIDTH // V7X_LANES):
            sl = slice(c * V7X_LANES, (c + 1) * V7X_LANES)
            qt_ref[0, sl, rows] = norm_rope(
                proj[:, sl], qg, HEAD_DIM ** -0.5 * LOG2_E).T.astype(BF16)

        k = norm_rope(proj[:, ATTN_WIDTH:ATTN_WIDTH + KV_WIDTH], kg, 1.0)
        k_ref[rows, :] = k.astype(BF16)

        vt = proj[:, ATTN_WIDTH + KV_WIDTH:ATTN_WIDTH + 2 * KV_WIDTH].T.astype(BF16)
        one = jnp.ones((HEAD_DIM, vt.shape[1]), BF16)
        for g in range(N_KV_HEADS):
            vxt_ref[0, g * V7X_LANES:g * V7X_LANES + HEAD_DIM, rows] = (
                vt[g * HEAD_DIM:(g + 1) * HEAD_DIM])
            vxt_ref[0, g * V7X_LANES + HEAD_DIM:(g + 1) * V7X_LANES, rows] = one

        u_ref[rows, :] = proj[:, ATTN_WIDTH + 2 * KV_WIDTH:].astype(BF16)

    n_sub = IN_TILE // IN_SUB
    nxt = project(0)
    for i in range(n_sub):
        cur = nxt
        if i + 1 < n_sub:
            nxt = project(i + 1)
        finish(i, cur)


def _in_proj(x2, mix_g, w_in, ones_bd, qg, kg, cos_t, sin_a, sin_b, w_out, w_up, w_down,
             batch, seq):
    n = x2.shape[0]
    steps = n // IN_TILE
    tiles_per_seq = seq // IN_TILE
    tok = lambda i: (i, 0)
    pos = lambda i: (i % tiles_per_seq, 0)
    chan_major = lambda i: (i // tiles_per_seq, 0, i % tiles_per_seq)
    slab = lambda w: pl.BlockSpec((w.shape[0] // steps, w.shape[1]), tok)
    bf16_like = lambda w: jax.ShapeDtypeStruct(w.shape, BF16)
    return pl.pallas_call(
        _in_proj_kernel,
        grid=(steps,),
        in_specs=[
            pl.BlockSpec((IN_TILE, D_MODEL), tok),
            _resident((1, D_MODEL)),
            _resident((D_MODEL, IN_PROJ_WIDTH)),
            _resident((V7X_LANES, V7X_LANES)),
            _resident((1, V7X_LANES)),
            _resident((1, V7X_LANES)),
            pl.BlockSpec((IN_TILE, V7X_LANES), pos),
            pl.BlockSpec((IN_TILE, V7X_LANES), pos),
            pl.BlockSpec((IN_TILE, V7X_LANES), pos),
            slab(w_out), slab(w_up), slab(w_down),
        ],
        out_specs=[
            pl.BlockSpec((1, ATTN_WIDTH, IN_TILE), chan_major),
            pl.BlockSpec((IN_TILE, KV_WIDTH), tok),
            pl.BlockSpec((1, N_KV_HEADS * V7X_LANES, IN_TILE), chan_major),
            pl.BlockSpec((IN_TILE, FOURIER_WIDTH), tok),
            slab(w_out), slab(w_up), slab(w_down),
        ],
        out_shape=[
            jax.ShapeDtypeStruct((batch, ATTN_WIDTH, seq), BF16),
            jax.ShapeDtypeStruct((n, KV_WIDTH), BF16),
            jax.ShapeDtypeStruct((batch, N_KV_HEADS * V7X_LANES, seq), BF16),
            jax.ShapeDtypeStruct((n, FOURIER_WIDTH), BF16),
            bf16_like(w_out), bf16_like(w_up), bf16_like(w_down),
        ],
        compiler_params=pltpu.CompilerParams(
            dimension_semantics=("arbitrary",), vmem_limit_bytes=VMEM_LIMIT),
        name="in_proj",
    )(x2, mix_g, w_in, ones_bd, qg, kg, cos_t, sin_a, sin_b, w_out, w_up, w_down)


def _fourier_kernel(u_ref, m1_ref, g2_ref, cc_ref, sc_ref, wbd_ref, y_ref, ma_ref, mb_ref):
    @pl.when(pl.program_id(0) == 0)
    def _():
        w = wbd_ref[...]
        ma_ref[...] = jnp.dot(cc_ref[...], w, precision=lax.Precision.HIGHEST,
                              preferred_element_type=F32).astype(BF16)
        mb_ref[...] = jnp.dot(sc_ref[...], w, precision=lax.Precision.HIGHEST,
                              preferred_element_type=F32).astype(BF16)

    x = u_ref[...].reshape(DFT_N1, DFT_N2, FOURIER_WIDTH)
    xt = jnp.swapaxes(x, 0, 1)

    m1 = m1_ref[...]
    q = []
    for n2 in range(DFT_N2):
        q.append(jnp.dot(m1, xt[n2], preferred_element_type=F32).astype(BF16))
    q = jnp.swapaxes(jnp.stack(q, axis=0), 0, 1)

    half = FOURIER_WIDTH // 2
    y = []
    for g in range(DFT_N1 // DFT_K1_GROUP):
        xr, xi = [], []
        for k1 in range(g * DFT_K1_GROUP, (g + 1) * DFT_K1_GROUP):
            qk = jnp.concatenate([q[k1], q[DFT_N1 + k1]], axis=0)
            xk = jnp.dot(g2_ref[k1], qk, preferred_element_type=F32)
            xr.append(xk[:DFT_N2].astype(BF16))
            xi.append(xk[DFT_N2:].astype(BF16))
        xr = jnp.concatenate(xr, axis=0)
        xi = jnp.concatenate(xi, axis=0)
        y.append(jnp.concatenate(
            [jnp.dot(xr[:, lanes], ma_ref[lanes, lanes], preferred_element_type=F32)
             + jnp.dot(xi[:, lanes], mb_ref[lanes, lanes], preferred_element_type=F32)
             for lanes in (slice(0, half), slice(half, FOURIER_WIDTH))], axis=1))
    y = jnp.concatenate(y, axis=0).astype(BF16).reshape(DFT_N1, DFT_N2, FOURIER_WIDTH)
    y_ref[...] = jnp.swapaxes(y, 0, 1).reshape(DFT_N1 * DFT_N2, FOURIER_WIDTH)


def _fourier(u, m1, g2, cc, sc, wbd, batch, seq):
    return pl.pallas_call(
        _fourier_kernel,
        grid=(batch,),
        in_specs=[
            pl.BlockSpec((seq, FOURIER_WIDTH), lambda b: (b, 0)),
            _resident((2 * DFT_N1, DFT_N1)),
            _resident((DFT_N1, 2 * DFT_N2, 2 * DFT_N2)),
            _resident((FOURIER_WIDTH, FOURIER_WIDTH)),
            _resident((FOURIER_WIDTH, FOURIER_WIDTH)),
            _resident((FOURIER_WIDTH, FOURIER_WIDTH)),
        ],
        out_specs=pl.BlockSpec((seq, FOURIER_WIDTH), lambda b: (b, 0)),
        out_shape=jax.ShapeDtypeStruct((batch * seq, FOURIER_WIDTH), BF16),
        scratch_shapes=[pltpu.VMEM((FOURIER_WIDTH, FOURIER_WIDTH), BF16),
                        pltpu.VMEM((FOURIER_WIDTH, FOURIER_WIDTH), BF16)],
        compiler_params=pltpu.CompilerParams(
            dimension_semantics=("arbitrary",), vmem_limit_bytes=VMEM_LIMIT),
        name="fourier",
    )(u, m1, g2, cc, sc, wbd)


def _attn_mlp_kernel(qt_ref, k_ref, vxt_ref, f_ref, x_ref, wo_ref, g2_ref, wu_ref, wd_ref,
                     g3_ref, o_ref, attn_ref):
    @pl.when(pl.program_id(0) == 0)
    def _():
        attn_ref[...] = jnp.zeros_like(attn_ref)

    def scores(h):
        q = qt_ref[0, h * HEAD_DIM:(h + 1) * HEAD_DIM, :]
        zero = jnp.zeros_like(q)
        g = h // (N_Q_HEADS // N_KV_HEADS)
        q_pad = jnp.concatenate([zero] * g + [q] + [zero] * (N_KV_HEADS - 1 - g), axis=0)
        return jnp.dot(k_ref[...], q_pad, preferred_element_type=F32)

    mlp = {}

    def mix():
        m = (jnp.dot(attn_ref[...], wo_ref[:ATTN_WIDTH, :], preferred_element_type=F32)
             + jnp.dot(f_ref[...], wo_ref[ATTN_WIDTH:, :], preferred_element_type=F32))
        x1 = x_ref[...] + m
        mlp["acc"] = x1
        mlp["h"] = (x1 * _rms_scale(x1) * g2_ref[...]).astype(BF16)

    def up(c):
        z = jnp.dot(mlp["h"], wu_ref[:, c * FF_CHUNK:(c + 1) * FF_CHUNK],
                    preferred_element_type=F32)
        mlp["z"] = jnp.square(jnp.maximum(z, 0.0)).astype(BF16)

    def down(c):
        mlp["acc"] = mlp["acc"] + jnp.dot(mlp["z"], wd_ref[c * FF_CHUNK:(c + 1) * FF_CHUNK, :],
                                         preferred_element_type=F32)

    def down_and_finish(c):
        wd = wd_ref[c * FF_CHUNK:(c + 1) * FF_CHUNK, :]
        for rows in (slice(0, Q_TILE // 2), slice(Q_TILE // 2, Q_TILE)):
            acc = mlp["acc"][rows] + jnp.dot(mlp["z"][rows], wd, preferred_element_type=F32)
            o_ref[rows, :] = acc * _rms_scale(acc) * g3_ref[...]

    units = []
    n_chunks = D_FF // FF_CHUNK
    for c in range(n_chunks):
        units += [functools.partial(up, c),
                  functools.partial(down if c + 1 < n_chunks else down_and_finish, c)]
    assert len(units) == N_Q_HEADS

    s_next = scores(0)
    mix()
    halves = []
    for h in range(N_Q_HEADS):
        s = s_next
        if h + 1 < N_Q_HEADS:
            s_next = scores(h + 1)
        units[h]()
        m = jnp.max(s, axis=0, keepdims=True)
        e = jnp.exp2(s - m).astype(BF16)
        g = h // (N_Q_HEADS // N_KV_HEADS)
        r = jnp.dot(vxt_ref[0, g * V7X_LANES:(g + 1) * V7X_LANES, :], e,
                    preferred_element_type=F32)
        halves.append(r[:HEAD_DIM] / r[HEAD_DIM:])
        if h % 2 == 1:
            c = h // 2
            pair = jnp.concatenate(halves, axis=0)
            attn_ref[:, c * V7X_LANES:(c + 1) * V7X_LANES] = pair.T.astype(BF16)
            halves = []


def _attn_mlp(qt, kx, vxt, four, x2, w_out, g2, w_up, w_down, g3, batch, seq):
    n = x2.shape[0]
    tiles = n // Q_TILE
    per_seq = seq // Q_TILE
    attn_tile = lambda i: jnp.minimum(i, tiles - 1)
    mlp_tile = lambda i: jnp.maximum(i - 1, 0)
    return pl.pallas_call(
        _attn_mlp_kernel,
        grid=(tiles + 1,),
        in_specs=[
            pl.BlockSpec((1, ATTN_WIDTH, Q_TILE),
                         lambda i: (attn_tile(i) // per_seq, 0, attn_tile(i) % per_seq)),
            pl.BlockSpec((seq, KV_WIDTH), lambda i: (attn_tile(i) // per_seq, 0)),
            pl.BlockSpec((1, N_KV_HEADS * V7X_LANES, seq),
                         lambda i: (attn_tile(i) // per_seq, 0, 0)),
            pl.BlockSpec((Q_TILE, FOURIER_WIDTH), lambda i: (mlp_tile(i), 0)),
            pl.BlockSpec((Q_TILE, D_MODEL), lambda i: (mlp_tile(i), 0)),
            _resident((D_MODEL, D_MODEL)),
            _resident((1, D_MODEL)),
            _resident((D_MODEL, D_FF)),
            _resident((D_FF, D_MODEL)),
            _resident((1, D_MODEL)),
        ],
        out_specs=pl.BlockSpec((Q_TILE, D_MODEL), lambda i: (mlp_tile(i), 0)),
        out_shape=jax.ShapeDtypeStruct((n, D_MODEL), F32),
        scratch_shapes=[pltpu.VMEM((Q_TILE, ATTN_WIDTH), BF16)],
        compiler_params=pltpu.CompilerParams(
            dimension_semantics=("arbitrary",), vmem_limit_bytes=FUSED_VMEM_LIMIT),
        name="attn_mlp",
    )(qt, kx, vxt, four, x2, w_out, g2, w_up, w_down, g3)


@functools.lru_cache(maxsize=None)
def _rope_tables(seq):
    rows = seq // GRID_W
    row = np.repeat(np.arange(rows), GRID_W).astype(np.float64)
    col = np.tile(np.arange(GRID_W), rows).astype(np.float64)
    inv_freq = ROPE_THETA ** (-np.arange(0, AXIS_DIM, 2, dtype=np.float64) / AXIS_DIM)
    row_ang = row[:, None] * inv_freq[None, :]
    col_ang = col[:, None] * inv_freq[None, :]
    ang = np.concatenate([row_ang, row_ang, col_ang, col_ang], axis=-1)
    ang = np.concatenate([ang, ang], axis=-1)
    first_half = (np.arange(V7X_LANES) % AXIS_DIM) < (AXIS_DIM // 2)
    sin = np.sin(ang)
    f32 = lambda a: a.astype(np.float32)
    return (f32(np.cos(ang)), f32(np.where(first_half, -sin, 0.0)),
            f32(np.where(first_half, 0.0, sin)))


@functools.lru_cache(maxsize=None)
def _dft_constants(seq):
    assert seq == DFT_N1 * DFT_N2
    k1 = np.arange(DFT_N1, dtype=np.int64)
    a1 = 2.0 * np.pi * ((k1[:, None] * k1[None, :]) % DFT_N1) / DFT_N1
    m1 = np.concatenate([np.cos(a1), -np.sin(a1)], axis=0)
    n2 = np.arange(DFT_N2, dtype=np.int64)
    a2 = 2.0 * np.pi * ((n2[:, None] * n2[None, :]) % DFT_N2) / DFT_N2
    c2, s2 = np.cos(a2), np.sin(a2)
    r2 = np.block([[c2, s2], [-s2, c2]])
    at = 2.0 * np.pi * (k1[:, None] * n2[None, :]) / seq
    g2 = np.stack([r2 @ np.block([[np.diag(np.cos(a)), np.diag(np.sin(a))],
                                  [-np.diag(np.sin(a)), np.diag(np.cos(a))]]) for a in at])
    c = np.arange(GROUP_DIM, dtype=np.int64)
    angc = 2.0 * np.pi * ((c[:, None] * c[None, :]) % GROUP_DIM) / GROUP_DIM
    scale = 1.0 / np.sqrt(float(seq * GROUP_DIM))
    eye = np.eye(N_GROUPS)
    cc = np.kron(eye, np.cos(angc)) * scale
    sc = np.kron(eye, np.sin(angc)) * scale
    ones_bd = np.kron(np.eye(V7X_LANES // HEAD_DIM), np.ones((HEAD_DIM, HEAD_DIM)))
    f32 = lambda a: a.astype(np.float32)
    return f32(m1), f32(g2), f32(cc), f32(sc), f32(ones_bd)


def kernel(x, mix_norm_g, w_in, q_norm_g, k_norm_g, w_fourier, w_out, mlp_norm_g, w_up,
           w_down, final_norm_g):
    batch, seq, d_model = x.shape
    assert d_model == D_MODEL and seq % Q_TILE == 0 and seq % IN_TILE == 0
    x2 = x.reshape(batch * seq, d_model)

    m1_np, g2_np, cc_np, sc_np, ones_np = _dft_constants(seq)
    m1 = jnp.asarray(m1_np).astype(BF16)
    g2 = jnp.asarray(g2_np).astype(BF16)
    cc, sc = jnp.asarray(cc_np), jnp.asarray(sc_np)
    ones_bd = jnp.asarray(ones_np, dtype=BF16)
    cos_t, sin_a, sin_b = (jnp.asarray(a) for a in _rope_tables(seq))

    row = lambda g: g.reshape(1, -1).astype(F32)
    two_heads = lambda g: jnp.tile(g.astype(F32), V7X_LANES // HEAD_DIM).reshape(1, V7X_LANES)
    wbd = (jnp.eye(N_GROUPS, dtype=F32)[:, None, :, None]
           * w_fourier.astype(F32)[:, :, None, :]).reshape(FOURIER_WIDTH, FOURIER_WIDTH)

    qt, kx, vxt, u, w_out16, w_up16, w_down16 = _in_proj(
        x2, row(mix_norm_g), w_in.astype(F32), ones_bd, two_heads(q_norm_g),
        two_heads(k_norm_g), cos_t, sin_a, sin_b, w_out.astype(F32), w_up.astype(F32),
        w_down.astype(F32), batch, seq)
    four = _fourier(u, m1, g2, cc, sc, wbd, batch, seq)
    out = _attn_mlp(qt, kx, vxt, four, x2, w_out16, row(mlp_norm_g), w_up16, w_down16,
                    row(final_norm_g), batch, seq)
    return out.reshape(batch, seq, d_model)
```

```python
import functools

import numpy as np
import jax
import jax.numpy as jnp
from jax import lax
from jax.experimental import pallas as pl
from jax.experimental.pallas import tpu as pltpu

D_MODEL = 1024
HEAD_DIM = 64
N_Q_HEADS = 8
N_KV_HEADS = 2
ATTN_WIDTH = N_Q_HEADS * HEAD_DIM
KV_WIDTH = N_KV_HEADS * HEAD_DIM
N_GROUPS = 8
GROUP_DIM = 64
FOURIER_WIDTH = N_GROUPS * GROUP_DIM
IN_PROJ_WIDTH = ATTN_WIDTH + 2 * KV_WIDTH + FOURIER_WIDTH
D_FF = 4 * D_MODEL
GRID_W = 64
AXIS_DIM = HEAD_DIM // 2
ROPE_THETA = 10000.0
NORM_EPS = 1e-6
LOG2_E = 1.4426950408889634

V7X_LANES = 128
V7X_VMEM_BYTES = 64 * 1024 * 1024

IN_TILE = 2048
IN_SUB = 256
Q_TILE = 512
FF_CHUNK = 1024
DFT_N1 = 32
DFT_N2 = 64
DFT_K1_GROUP = 8
VMEM_LIMIT = 56 * 1024 * 1024
FUSED_VMEM_LIMIT = 60 * 1024 * 1024

BF16 = jnp.bfloat16
F32 = jnp.float32


def _resident(shape):
    zeros = (0,) * len(shape)
    return pl.BlockSpec(shape, lambda *_: zeros, pipeline_mode=pl.Buffered(1))


def _rms_scale(x):
    return lax.rsqrt(jnp.mean(x * x, axis=-1, keepdims=True) + NORM_EPS)


def _in_proj_kernel(x_ref, g_ref, w_ref, ones_ref, onesk_ref, qga_ref, qgb_ref, kg_ref,
                    cos_ref, sin_ref, sink_ref, wo_ref, wu_ref, wd_ref,
                    qt_ref, k_ref, vxt_ref, u_ref, wo16_ref, wu16_ref, wd16_ref):
    wo16_ref[...] = wo_ref[...].astype(BF16)
    wu16_ref[...] = wu_ref[...].astype(BF16)
    wd16_ref[...] = wd_ref[...].astype(BF16)

    g = g_ref[...]
    ones = ones_ref[...]
    ones_k = onesk_ref[...]
    qga = qga_ref[...]
    qgb = qgb_ref[...]
    kg = kg_ref[...]
    w = w_ref[...].astype(BF16)

    def project(i):
        x = x_ref[i * IN_SUB:(i + 1) * IN_SUB, :]
        h = x * _rms_scale(x) * g
        return jnp.dot(h.astype(BF16), w, preferred_element_type=F32)

    def finish(i, proj):
        rows = slice(i * IN_SUB, (i + 1) * IN_SUB)
        cos = cos_ref[rows, :]
        sin = sin_ref[rows, :]

        def inv_rms(squares, ones):
            ss = jnp.dot(squares.astype(BF16), ones, preferred_element_type=F32)
            return lax.rsqrt(ss * (1.0 / HEAD_DIM) + NORM_EPS)

        half = ATTN_WIDTH // 2
        for c in range(half // V7X_LANES):
            la = slice(c * V7X_LANES, (c + 1) * V7X_LANES)
            lb = slice(half + c * V7X_LANES, half + (c + 1) * V7X_LANES)
            a, b = proj[:, la], proj[:, lb]
            inv = inv_rms(a * a + b * b, ones) * (HEAD_DIM ** -0.5 * LOG2_E)
            ya, yb = a * inv * qga, b * inv * qgb
            qt_ref[0, la, rows] = (ya * cos - yb * sin).T.astype(BF16)
            qt_ref[0, lb, rows] = (yb * cos + ya * sin).T.astype(BF16)

        k = proj[:, ATTN_WIDTH:ATTN_WIDTH + KV_WIDTH]
        yk = k * inv_rms(k * k, ones_k) * kg
        k_ref[rows, :] = (yk * cos + pltpu.roll(yk, HEAD_DIM, 1) * sink_ref[rows, :]).astype(BF16)

        vt = proj[:, ATTN_WIDTH + KV_WIDTH:ATTN_WIDTH + 2 * KV_WIDTH].T.astype(BF16)
        one = jnp.ones((HEAD_DIM, vt.shape[1]), BF16)
        for g in range(N_KV_HEADS):
            vxt_ref[0, g * V7X_LANES:g * V7X_LANES + HEAD_DIM, rows] = (
                vt[g * HEAD_DIM:(g + 1) * HEAD_DIM])
            vxt_ref[0, g * V7X_LANES + HEAD_DIM:(g + 1) * V7X_LANES, rows] = one

        u_ref[rows, :] = proj[:, ATTN_WIDTH + 2 * KV_WIDTH:].astype(BF16)

    n_sub = IN_TILE // IN_SUB
    nxt = project(0)
    for i in range(n_sub):
        cur = nxt
        if i + 1 < n_sub:
            nxt = project(i + 1)
        finish(i, cur)


def _in_proj(x2, mix_g, w_in, ones_q, ones_k, qga, qgb, kg, cos_t, sin_t, sin_k,
             w_out, w_up, w_down, batch, seq):
    n = x2.shape[0]
    steps = n // IN_TILE
    tiles_per_seq = seq // IN_TILE
    tok = lambda i: (i, 0)
    pos = lambda i: (i % tiles_per_seq, 0)
    chan_major = lambda i: (i // tiles_per_seq, 0, i % tiles_per_seq)
    slab = lambda w: pl.BlockSpec((w.shape[0] // steps, w.shape[1]), tok)
    bf16_like = lambda w: jax.ShapeDtypeStruct(w.shape, BF16)
    return pl.pallas_call(
        _in_proj_kernel,
        grid=(steps,),
        in_specs=[
            pl.BlockSpec((IN_TILE, D_MODEL), tok),
            _resident((1, D_MODEL)),
            _resident((D_MODEL, IN_PROJ_WIDTH)),
            _resident((V7X_LANES, V7X_LANES)),
            _resident((V7X_LANES, V7X_LANES)),
            _resident((1, V7X_LANES)),
            _resident((1, V7X_LANES)),
            _resident((1, V7X_LANES)),
            pl.BlockSpec((IN_TILE, V7X_LANES), pos),
            pl.BlockSpec((IN_TILE, V7X_LANES), pos),
            pl.BlockSpec((IN_TILE, V7X_LANES), pos),
            slab(w_out), slab(w_up), slab(w_down),
        ],
        out_specs=[
            pl.BlockSpec((1, ATTN_WIDTH, IN_TILE), chan_major),
            pl.BlockSpec((IN_TILE, KV_WIDTH), tok),
            pl.BlockSpec((1, N_KV_HEADS * V7X_LANES, IN_TILE), chan_major),
            pl.BlockSpec((IN_TILE, FOURIER_WIDTH), tok),
            slab(w_out), slab(w_up), slab(w_down),
        ],
        out_shape=[
            jax.ShapeDtypeStruct((batch, ATTN_WIDTH, seq), BF16),
            jax.ShapeDtypeStruct((n, KV_WIDTH), BF16),
            jax.ShapeDtypeStruct((batch, N_KV_HEADS * V7X_LANES, seq), BF16),
            jax.ShapeDtypeStruct((n, FOURIER_WIDTH), BF16),
            bf16_like(w_out), bf16_like(w_up), bf16_like(w_down),
        ],
        compiler_params=pltpu.CompilerParams(
            dimension_semantics=("arbitrary",), vmem_limit_bytes=VMEM_LIMIT),
        name="in_proj",
    )(x2, mix_g, w_in, ones_q, ones_k, qga, qgb, kg, cos_t, sin_t, sin_k, w_out, w_up, w_down)


def _fourier_kernel(u_ref, m1_ref, g2_ref, cc_ref, sc_ref, wbd_ref, y_ref, ma_ref, mb_ref):
    @pl.when(pl.program_id(0) == 0)
    def _():
        w = wbd_ref[...]
        ma_ref[...] = jnp.dot(cc_ref[...], w, precision=lax.Precision.HIGHEST,
                              preferred_element_type=F32).astype(BF16)
        mb_ref[...] = jnp.dot(sc_ref[...], w, precision=lax.Precision.HIGHEST,
                              preferred_element_type=F32).astype(BF16)

    x = u_ref[...].reshape(DFT_N1, DFT_N2, FOURIER_WIDTH)
    xt = jnp.swapaxes(x, 0, 1)

    m1 = m1_ref[...]
    q = []
    for n2 in range(DFT_N2):
        q.append(jnp.dot(m1, xt[n2], preferred_element_type=F32).astype(BF16))
    q = jnp.swapaxes(jnp.stack(q, axis=0), 0, 1)

    half = FOURIER_WIDTH // 2
    y = []
    for g in range(DFT_N1 // DFT_K1_GROUP):
        xr, xi = [], []
        for k1 in range(g * DFT_K1_GROUP, (g + 1) * DFT_K1_GROUP):
            qk = jnp.concatenate([q[k1], q[DFT_N1 + k1]], axis=0)
            xk = jnp.dot(g2_ref[k1], qk, preferred_element_type=F32)
            xr.append(xk[:DFT_N2].astype(BF16))
            xi.append(xk[DFT_N2:].astype(BF16))
        xr = jnp.concatenate(xr, axis=0)
        xi = jnp.concatenate(xi, axis=0)
        y.append(jnp.concatenate(
            [jnp.dot(xr[:, lanes], ma_ref[lanes, lanes], preferred_element_type=F32)
             + jnp.dot(xi[:, lanes], mb_ref[lanes, lanes], preferred_element_type=F32)
             for lanes in (slice(0, half), slice(half, FOURIER_WIDTH))], axis=1))
    y = jnp.concatenate(y, axis=0).astype(BF16).reshape(DFT_N1, DFT_N2, FOURIER_WIDTH)
    y_ref[...] = jnp.swapaxes(y, 0, 1).reshape(DFT_N1 * DFT_N2, FOURIER_WIDTH)


def _fourier(u, m1, g2, cc, sc, wbd, batch, seq):
    return pl.pallas_call(
        _fourier_kernel,
        grid=(batch,),
        in_specs=[
            pl.BlockSpec((seq, FOURIER_WIDTH), lambda b: (b, 0)),
            _resident((2 * DFT_N1, DFT_N1)),
            _resident((DFT_N1, 2 * DFT_N2, 2 * DFT_N2)),
            _resident((FOURIER_WIDTH, FOURIER_WIDTH)),
            _resident((FOURIER_WIDTH, FOURIER_WIDTH)),
            _resident((FOURIER_WIDTH, FOURIER_WIDTH)),
        ],
        out_specs=pl.BlockSpec((seq, FOURIER_WIDTH), lambda b: (b, 0)),
        out_shape=jax.ShapeDtypeStruct((batch * seq, FOURIER_WIDTH), BF16),
        scratch_shapes=[pltpu.VMEM((FOURIER_WIDTH, FOURIER_WIDTH), BF16),
                        pltpu.VMEM((FOURIER_WIDTH, FOURIER_WIDTH), BF16)],
        compiler_params=pltpu.CompilerParams(
            dimension_semantics=("arbitrary",), vmem_limit_bytes=VMEM_LIMIT),
        name="fourier",
    )(u, m1, g2, cc, sc, wbd)


def _attn_mlp_kernel(qt_ref, k_ref, vxt_ref, f_ref, x_ref, wo_ref, g2_ref, wu_ref, wd_ref,
                     g3_ref, o_ref, attn_ref):
    @pl.when(pl.program_id(0) == 0)
    def _():
        attn_ref[...] = jnp.zeros_like(attn_ref)

    def scores(h):
        hw = HEAD_DIM // 2
        first = qt_ref[0, h * hw:(h + 1) * hw, :]
        second = qt_ref[0, ATTN_WIDTH // 2 + h * hw:ATTN_WIDTH // 2 + (h + 1) * hw, :]
        zero = jnp.zeros_like(first)
        g = h // (N_Q_HEADS // N_KV_HEADS)
        pad = lambda part: [zero] * g + [part] + [zero] * (N_KV_HEADS - 1 - g)
        q_pad = jnp.concatenate(pad(first) + pad(second), axis=0)
        return jnp.dot(k_ref[...], q_pad, preferred_element_type=F32)

    mlp = {}

    def mix():
        m = (jnp.dot(attn_ref[...], wo_ref[:ATTN_WIDTH, :], preferred_element_type=F32)
             + jnp.dot(f_ref[...], wo_ref[ATTN_WIDTH:, :], preferred_element_type=F32))
        x1 = x_ref[...] + m
        mlp["acc"] = x1
        mlp["h"] = (x1 * _rms_scale(x1) * g2_ref[...]).astype(BF16)

    def up(c):
        z = jnp.dot(mlp["h"], wu_ref[:, c * FF_CHUNK:(c + 1) * FF_CHUNK],
                    preferred_element_type=F32)
        mlp["z"] = jnp.square(jnp.maximum(z, 0.0)).astype(BF16)

    def down(c):
        mlp["acc"] = mlp["acc"] + jnp.dot(mlp["z"], wd_ref[c * FF_CHUNK:(c + 1) * FF_CHUNK, :],
                                         preferred_element_type=F32)

    def down_and_finish(c):
        wd = wd_ref[c * FF_CHUNK:(c + 1) * FF_CHUNK, :]
        for rows in (slice(0, Q_TILE // 2), slice(Q_TILE // 2, Q_TILE)):
            acc = mlp["acc"][rows] + jnp.dot(mlp["z"][rows], wd, preferred_element_type=F32)
            o_ref[rows, :] = acc * _rms_scale(acc) * g3_ref[...]

    units = []
    n_chunks = D_FF // FF_CHUNK
    for c in range(n_chunks):
        units += [functools.partial(up, c),
                  functools.partial(down if c + 1 < n_chunks else down_and_finish, c)]
    assert len(units) == N_Q_HEADS

    s_next = scores(0)
    mix()
    halves = []
    for h in range(N_Q_HEADS):
        s = s_next
        if h + 1 < N_Q_HEADS:
            s_next = scores(h + 1)
        units[h]()
        m = jnp.max(s, axis=0, keepdims=True)
        e = jnp.exp2(s - m).astype(BF16)
        g = h // (N_Q_HEADS // N_KV_HEADS)
        r = jnp.dot(vxt_ref[0, g * V7X_LANES:(g + 1) * V7X_LANES, :], e,
                    preferred_element_type=F32)
        halves.append(r[:HEAD_DIM] / r[HEAD_DIM:])
        if h % 2 == 1:
            c = h // 2
            pair = jnp.concatenate(halves, axis=0)
            attn_ref[:, c * V7X_LANES:(c + 1) * V7X_LANES] = pair.T.astype(BF16)
            halves = []


def _attn_mlp(qt, k, vxt, four, x2, w_out, g2, w_up, w_down, g3, batch, seq):
    n = x2.shape[0]
    tiles = n // Q_TILE
    per_seq = seq // Q_TILE
    attn_tile = lambda i: jnp.minimum(i, tiles - 1)
    mlp_tile = lambda i: jnp.maximum(i - 1, 0)
    return pl.pallas_call(
        _attn_mlp_kernel,
        grid=(tiles + 1,),
        in_specs=[
            pl.BlockSpec((1, ATTN_WIDTH, Q_TILE),
                         lambda i: (attn_tile(i) // per_seq, 0, attn_tile(i) % per_seq)),
            pl.BlockSpec((seq, KV_WIDTH), lambda i: (attn_tile(i) // per_seq, 0)),
            pl.BlockSpec((1, N_KV_HEADS * V7X_LANES, seq),
                         lambda i: (attn_tile(i) // per_seq, 0, 0)),
            pl.BlockSpec((Q_TILE, FOURIER_WIDTH), lambda i: (mlp_tile(i), 0)),
            pl.BlockSpec((Q_TILE, D_MODEL), lambda i: (mlp_tile(i), 0)),
            _resident((D_MODEL, D_MODEL)),
            _resident((1, D_MODEL)),
            _resident((D_MODEL, D_FF)),
            _resident((D_FF, D_MODEL)),
            _resident((1, D_MODEL)),
        ],
        out_specs=pl.BlockSpec((Q_TILE, D_MODEL), lambda i: (mlp_tile(i), 0)),
        out_shape=jax.ShapeDtypeStruct((n, D_MODEL), F32),
        scratch_shapes=[pltpu.VMEM((Q_TILE, ATTN_WIDTH), BF16)],
        compiler_params=pltpu.CompilerParams(
            dimension_semantics=("arbitrary",), vmem_limit_bytes=FUSED_VMEM_LIMIT),
        name="attn_mlp",
    )(qt, k, vxt, four, x2, w_out, g2, w_up, w_down, g3)


@functools.lru_cache(maxsize=None)
def _qk_layout():
    quarter = AXIS_DIM // 2
    firsts = np.concatenate([np.arange(quarter), AXIS_DIM + np.arange(quarter)])
    seconds = firsts + quarter
    q_cols = np.concatenate([HEAD_DIM * h + part for part in (firsts, seconds)
                             for h in range(N_Q_HEADS)])
    k_cols = ATTN_WIDTH + np.concatenate([HEAD_DIM * h + part for part in (firsts, seconds)
                                          for h in range(N_KV_HEADS)])
    perm = np.concatenate([q_cols, k_cols])
    per_block = V7X_LANES // AXIS_DIM
    q_first_dims, q_second_dims = np.tile(firsts, per_block), np.tile(seconds, per_block)
    k_dims = np.concatenate([np.tile(firsts, N_KV_HEADS), np.tile(seconds, N_KV_HEADS)])
    q_head = np.arange(V7X_LANES) // AXIS_DIM
    k_head = (np.arange(V7X_LANES) // AXIS_DIM) % N_KV_HEADS
    same = lambda head: (head[:, None] == head[None, :]).astype(np.float32)
    return perm, q_first_dims, q_second_dims, k_dims, same(q_head), same(k_head)


@functools.lru_cache(maxsize=None)
def _rope_tables(seq):
    rows = seq // GRID_W
    row = np.repeat(np.arange(rows), GRID_W).astype(np.float64)
    col = np.tile(np.arange(GRID_W), rows).astype(np.float64)
    inv_freq = ROPE_THETA ** (-np.arange(0, AXIS_DIM, 2, dtype=np.float64) / AXIS_DIM)
    ang = np.concatenate([row[:, None] * inv_freq[None, :], col[:, None] * inv_freq[None, :]],
                         axis=-1)
    ang = np.tile(ang, (1, V7X_LANES // AXIS_DIM))
    sin = np.sin(ang)
    k_first = np.arange(V7X_LANES) < HEAD_DIM
    f32 = lambda a: a.astype(np.float32)
    return f32(np.cos(ang)), f32(sin), f32(np.where(k_first, -sin, sin))


@functools.lru_cache(maxsize=None)
def _dft_constants(seq):
    assert seq == DFT_N1 * DFT_N2
    k1 = np.arange(DFT_N1, dtype=np.int64)
    a1 = 2.0 * np.pi * ((k1[:, None] * k1[None, :]) % DFT_N1) / DFT_N1
    m1 = np.concatenate([np.cos(a1), -np.sin(a1)], axis=0)
    n2 = np.arange(DFT_N2, dtype=np.int64)
    a2 = 2.0 * np.pi * ((n2[:, None] * n2[None, :]) % DFT_N2) / DFT_N2
    c2, s2 = np.cos(a2), np.sin(a2)
    r2 = np.block([[c2, s2], [-s2, c2]])
    at = 2.0 * np.pi * (k1[:, None] * n2[None, :]) / seq
    g2 = np.stack([r2 @ np.block([[np.diag(np.cos(a)), np.diag(np.sin(a))],
                                  [-np.diag(np.sin(a)), np.diag(np.cos(a))]]) for a in at])
    c = np.arange(GROUP_DIM, dtype=np.int64)
    angc = 2.0 * np.pi * ((c[:, None] * c[None, :]) % GROUP_DIM) / GROUP_DIM
    scale = 1.0 / np.sqrt(float(seq * GROUP_DIM))
    eye = np.eye(N_GROUPS)
    cc = np.kron(eye, np.cos(angc)) * scale
    sc = np.kron(eye, np.sin(angc)) * scale
    f32 = lambda a: a.astype(np.float32)
    return f32(m1), f32(g2), f32(cc), f32(sc)


def kernel(x, mix_norm_g, w_in, q_norm_g, k_norm_g, w_fourier, w_out, mlp_norm_g, w_up,
           w_down, final_norm_g):
    batch, seq, d_model = x.shape
    assert d_model == D_MODEL and seq % Q_TILE == 0 and seq % IN_TILE == 0
    x2 = x.reshape(batch * seq, d_model)

    m1_np, g2_np, cc_np, sc_np = _dft_constants(seq)
    m1 = jnp.asarray(m1_np).astype(BF16)
    g2 = jnp.asarray(g2_np).astype(BF16)
    cc, sc = jnp.asarray(cc_np), jnp.asarray(sc_np)
    cos_t, sin_t, sin_k = (jnp.asarray(a) for a in _rope_tables(seq))

    row = lambda g: g.reshape(1, -1).astype(F32)
    wbd = (jnp.eye(N_GROUPS, dtype=F32)[:, None, :, None]
           * w_fourier.astype(F32)[:, :, None, :]).reshape(FOURIER_WIDTH, FOURIER_WIDTH)

    perm, q_first, q_second, k_dims, same_q, same_k = _qk_layout()
    w_in_f32 = w_in.astype(F32)
    w_in_perm = jnp.concatenate([w_in_f32[:, perm], w_in_f32[:, perm.size:]], axis=1)
    lane_gain = lambda g, dims: g.astype(F32)[dims].reshape(1, V7X_LANES)

    qt, k, vxt, u, w_out16, w_up16, w_down16 = _in_proj(
        x2, row(mix_norm_g), w_in_perm, jnp.asarray(same_q, dtype=BF16),
        jnp.asarray(same_k, dtype=BF16), lane_gain(q_norm_g, q_first),
        lane_gain(q_norm_g, q_second), lane_gain(k_norm_g, k_dims), cos_t, sin_t, sin_k,
        w_out.astype(F32), w_up.astype(F32), w_down.astype(F32), batch, seq)
    four = _fourier(u, m1, g2, cc, sc, wbd, batch, seq)
    out = _attn_mlp(qt, k, vxt, four, x2, w_out16, row(mlp_norm_g), w_up16, w_down16,
                    row(final_norm_g), batch, seq)
    return out.reshape(batch, seq, d_model)
```

```python
import functools

import numpy as np
import jax
import jax.numpy as jnp
from jax import lax
from jax.experimental import pallas as pl
from jax.experimental.pallas import tpu as pltpu

D_MODEL = 1024
HEAD_DIM = 64
N_Q_HEADS = 8
N_KV_HEADS = 2
ATTN_WIDTH = N_Q_HEADS * HEAD_DIM
KV_WIDTH = N_KV_HEADS * HEAD_DIM
N_GROUPS = 8
GROUP_DIM = 64
FOURIER_WIDTH = N_GROUPS * GROUP_DIM
IN_PROJ_WIDTH = ATTN_WIDTH + 2 * KV_WIDTH + FOURIER_WIDTH
D_FF = 4 * D_MODEL
GRID_W = 64
AXIS_DIM = HEAD_DIM // 2
ROPE_THETA = 10000.0
NORM_EPS = 1e-6
LOG2_E = 1.4426950408889634

V7X_LANES = 128
V7X_VMEM_BYTES = 64 * 1024 * 1024

IN_TILE = 2048
IN_SUB = 256
Q_TILE = 512
FF_CHUNK = 1024
DFT_N1 = 32
DFT_N2 = 64
DFT_K1_GROUP = 8
VMEM_LIMIT = 56 * 1024 * 1024
FUSED_VMEM_LIMIT = 60 * 1024 * 1024

BF16 = jnp.bfloat16
F32 = jnp.float32


def _resident(shape):
    zeros = (0,) * len(shape)
    return pl.BlockSpec(shape, lambda *_: zeros, pipeline_mode=pl.Buffered(1))


def _rms_scale(x):
    return lax.rsqrt(jnp.mean(x * x, axis=-1, keepdims=True) + NORM_EPS)


def _in_proj_kernel(x_ref, g_ref, w_ref, perm_ref, ones_ref, onesk_ref, qga_ref, qgb_ref,
                    kg_ref, cos_ref, sin_ref, sink_ref, wo_ref, wu_ref, wd_ref,
                    qt_ref, k_ref, vxt_ref, u_ref, wo16_ref, wu16_ref, wd16_ref, w16_ref):
    wo16_ref[...] = wo_ref[...].astype(BF16)
    wu16_ref[...] = wu_ref[...].astype(BF16)
    wd16_ref[...] = wd_ref[...].astype(BF16)

    @pl.when(pl.program_id(0) == 0)
    def _():
        n_qk = perm_ref.shape[0]
        w16_ref[:, :n_qk] = jnp.dot(w_ref[:, :n_qk].astype(BF16), perm_ref[...],
                                    preferred_element_type=F32).astype(BF16)
        w16_ref[:, n_qk:] = w_ref[:, n_qk:].astype(BF16)

    g = g_ref[...]
    ones = ones_ref[...]
    ones_k = onesk_ref[...]
    qga = qga_ref[...]
    qgb = qgb_ref[...]
    kg = kg_ref[...]
    w = w16_ref[...]

    def project(i):
        x = x_ref[i * IN_SUB:(i + 1) * IN_SUB, :]
        h = x * _rms_scale(x) * g
        return jnp.dot(h.astype(BF16), w, preferred_element_type=F32)

    def finish(i, proj):
        rows = slice(i * IN_SUB, (i + 1) * IN_SUB)
        cos = cos_ref[rows, :]
        sin = sin_ref[rows, :]

        def inv_rms(squares, ones):
            ss = jnp.dot(squares.astype(BF16), ones, preferred_element_type=F32)
            return lax.rsqrt(ss * (1.0 / HEAD_DIM) + NORM_EPS)

        half = ATTN_WIDTH // 2
        for c in range(half // V7X_LANES):
            la = slice(c * V7X_LANES, (c + 1) * V7X_LANES)
            lb = slice(half + c * V7X_LANES, half + (c + 1) * V7X_LANES)
            a, b = proj[:, la], proj[:, lb]
            inv = inv_rms(a * a + b * b, ones) * (HEAD_DIM ** -0.5 * LOG2_E)
            ya, yb = a * inv * qga, b * inv * qgb
            qt_ref[0, la, rows] = (ya * cos - yb * sin).T.astype(BF16)
            qt_ref[0, lb, rows] = (yb * cos + ya * sin).T.astype(BF16)

        k = proj[:, ATTN_WIDTH:ATTN_WIDTH + KV_WIDTH]
        yk = k * inv_rms(k * k, ones_k) * kg
        k_ref[rows, :] = (yk * cos + pltpu.roll(yk, HEAD_DIM, 1) * sink_ref[rows, :]).astype(BF16)

        vt = proj[:, ATTN_WIDTH + KV_WIDTH:ATTN_WIDTH + 2 * KV_WIDTH].T.astype(BF16)
        one = jnp.ones((HEAD_DIM, vt.shape[1]), BF16)
        for g in range(N_KV_HEADS):
            vxt_ref[0, g * V7X_LANES:g * V7X_LANES + HEAD_DIM, rows] = (
                vt[g * HEAD_DIM:(g + 1) * HEAD_DIM])
            vxt_ref[0, g * V7X_LANES + HEAD_DIM:(g + 1) * V7X_LANES, rows] = one

        u_ref[rows, :] = proj[:, ATTN_WIDTH + 2 * KV_WIDTH:].astype(BF16)

    n_sub = IN_TILE // IN_SUB
    nxt = project(0)
    for i in range(n_sub):
        cur = nxt
        if i + 1 < n_sub:
            nxt = project(i + 1)
        finish(i, cur)


def _in_proj(x2, mix_g, w_in, perm, ones_q, ones_k, qga, qgb, kg, cos_t, sin_t, sin_k,
             w_out, w_up, w_down, batch, seq):
    n = x2.shape[0]
    steps = n // IN_TILE
    tiles_per_seq = seq // IN_TILE
    tok = lambda i: (i, 0)
    pos = lambda i: (i % tiles_per_seq, 0)
    chan_major = lambda i: (i // tiles_per_seq, 0, i % tiles_per_seq)
    slab = lambda w: pl.BlockSpec((w.shape[0] // steps, w.shape[1]), tok)
    bf16_like = lambda w: jax.ShapeDtypeStruct(w.shape, BF16)
    return pl.pallas_call(
        _in_proj_kernel,
        grid=(steps,),
        in_specs=[
            pl.BlockSpec((IN_TILE, D_MODEL), tok),
            _resident((1, D_MODEL)),
            _resident((D_MODEL, IN_PROJ_WIDTH)),
            _resident(perm.shape),
            _resident((V7X_LANES, V7X_LANES)),
            _resident((V7X_LANES, V7X_LANES)),
            _resident((1, V7X_LANES)),
            _resident((1, V7X_LANES)),
            _resident((1, V7X_LANES)),
            pl.BlockSpec((IN_TILE, V7X_LANES), pos),
            pl.BlockSpec((IN_TILE, V7X_LANES), pos),
            pl.BlockSpec((IN_TILE, V7X_LANES), pos),
            slab(w_out), slab(w_up), slab(w_down),
        ],
        out_specs=[
            pl.BlockSpec((1, ATTN_WIDTH, IN_TILE), chan_major),
            pl.BlockSpec((IN_TILE, KV_WIDTH), tok),
            pl.BlockSpec((1, N_KV_HEADS * V7X_LANES, IN_TILE), chan_major),
            pl.BlockSpec((IN_TILE, FOURIER_WIDTH), tok),
            slab(w_out), slab(w_up), slab(w_down),
        ],
        out_shape=[
            jax.ShapeDtypeStruct((batch, ATTN_WIDTH, seq), BF16),
            jax.ShapeDtypeStruct((n, KV_WIDTH), BF16),
            jax.ShapeDtypeStruct((batch, N_KV_HEADS * V7X_LANES, seq), BF16),
            jax.ShapeDtypeStruct((n, FOURIER_WIDTH), BF16),
            bf16_like(w_out), bf16_like(w_up), bf16_like(w_down),
        ],
        scratch_shapes=[pltpu.VMEM((D_MODEL, IN_PROJ_WIDTH), BF16)],
        compiler_params=pltpu.CompilerParams(
            dimension_semantics=("arbitrary",), vmem_limit_bytes=VMEM_LIMIT),
        name="in_proj",
    )(x2, mix_g, w_in, perm, ones_q, ones_k, qga, qgb, kg, cos_t, sin_t, sin_k,
      w_out, w_up, w_down)


def _fourier_kernel(u_ref, m1_ref, g2_ref, cc_ref, sc_ref, wbd_ref, y_ref, ma_ref, mb_ref):
    @pl.when(pl.program_id(0) == 0)
    def _():
        w = wbd_ref[...]
        ma_ref[...] = jnp.dot(cc_ref[...], w, precision=lax.Precision.HIGHEST,
                              preferred_element_type=F32).astype(BF16)
        mb_ref[...] = jnp.dot(sc_ref[...], w, precision=lax.Precision.HIGHEST,
                              preferred_element_type=F32).astype(BF16)

    x = u_ref[...].reshape(DFT_N1, DFT_N2, FOURIER_WIDTH)
    xt = jnp.swapaxes(x, 0, 1)

    m1 = m1_ref[...]
    q = []
    for n2 in range(DFT_N2):
        q.append(jnp.dot(m1, xt[n2], preferred_element_type=F32).astype(BF16))
    q = jnp.swapaxes(jnp.stack(q, axis=0), 0, 1)

    half = FOURIER_WIDTH // 2
    y = []
    for g in range(DFT_N1 // DFT_K1_GROUP):
        xr, xi = [], []
        for k1 in range(g * DFT_K1_GROUP, (g + 1) * DFT_K1_GROUP):
            qk = jnp.concatenate([q[k1], q[DFT_N1 + k1]], axis=0)
            xk = jnp.dot(g2_ref[k1], qk, preferred_element_type=F32)
            xr.append(xk[:DFT_N2].astype(BF16))
            xi.append(xk[DFT_N2:].astype(BF16))
        xr = jnp.concatenate(xr, axis=0)
        xi = jnp.concatenate(xi, axis=0)
        y.append(jnp.concatenate(
            [jnp.dot(xr[:, lanes], ma_ref[lanes, lanes], preferred_element_type=F32)
             + jnp.dot(xi[:, lanes], mb_ref[lanes, lanes], preferred_element_type=F32)
             for lanes in (slice(0, half), slice(half, FOURIER_WIDTH))], axis=1))
    y = jnp.concatenate(y, axis=0).astype(BF16).reshape(DFT_N1, DFT_N2, FOURIER_WIDTH)
    y_ref[...] = jnp.swapaxes(y, 0, 1).reshape(DFT_N1 * DFT_N2, FOURIER_WIDTH)


def _fourier(u, m1, g2, cc, sc, wbd, batch, seq):
    return pl.pallas_call(
        _fourier_kernel,
        grid=(batch,),
        in_specs=[
            pl.BlockSpec((seq, FOURIER_WIDTH), lambda b: (b, 0)),
            _resident((2 * DFT_N1, DFT_N1)),
            _resident((DFT_N1, 2 * DFT_N2, 2 * DFT_N2)),
            _resident((FOURIER_WIDTH, FOURIER_WIDTH)),
            _resident((FOURIER_WIDTH, FOURIER_WIDTH)),
            _resident((FOURIER_WIDTH, FOURIER_WIDTH)),
        ],
        out_specs=pl.BlockSpec((seq, FOURIER_WIDTH), lambda b: (b, 0)),
        out_shape=jax.ShapeDtypeStruct((batch * seq, FOURIER_WIDTH), BF16),
        scratch_shapes=[pltpu.VMEM((FOURIER_WIDTH, FOURIER_WIDTH), BF16),
                        pltpu.VMEM((FOURIER_WIDTH, FOURIER_WIDTH), BF16)],
        compiler_params=pltpu.CompilerParams(
            dimension_semantics=("arbitrary",), vmem_limit_bytes=VMEM_LIMIT),
        name="fourier",
    )(u, m1, g2, cc, sc, wbd)


def _attn_mlp_kernel(qt_ref, k_ref, vxt_ref, f_ref, x_ref, wo_ref, g2_ref, wu_ref, wd_ref,
                     g3_ref, o_ref, attn_ref):
    def scores(h):
        hw = HEAD_DIM // 2
        first = qt_ref[0, h * hw:(h + 1) * hw, :]
        second = qt_ref[0, ATTN_WIDTH // 2 + h * hw:ATTN_WIDTH // 2 + (h + 1) * hw, :]
        zero = jnp.zeros_like(first)
        g = h // (N_Q_HEADS // N_KV_HEADS)
        pad = lambda part: [zero] * g + [part] + [zero] * (N_KV_HEADS - 1 - g)
        q_pad = jnp.concatenate(pad(first) + pad(second), axis=0)
        return jnp.dot(k_ref[...], q_pad, preferred_element_type=F32)

    def step(attention, mlp_pass):
        mlp = {}

        def mix():
            m = (jnp.dot(attn_ref[...], wo_ref[:ATTN_WIDTH, :], preferred_element_type=F32)
                 + jnp.dot(f_ref[...], wo_ref[ATTN_WIDTH:, :], preferred_element_type=F32))
            x1 = x_ref[...] + m
            mlp["acc"] = x1
            mlp["h"] = (x1 * _rms_scale(x1) * g2_ref[...]).astype(BF16)

        def up(c):
            z = jnp.dot(mlp["h"], wu_ref[:, c * FF_CHUNK:(c + 1) * FF_CHUNK],
                        preferred_element_type=F32)
            mlp["z"] = jnp.square(jnp.maximum(z, 0.0)).astype(BF16)

        def down(c):
            mlp["acc"] = mlp["acc"] + jnp.dot(
                mlp["z"], wd_ref[c * FF_CHUNK:(c + 1) * FF_CHUNK, :], preferred_element_type=F32)

        def down_and_finish(c):
            wd = wd_ref[c * FF_CHUNK:(c + 1) * FF_CHUNK, :]
            for rows in (slice(0, Q_TILE // 2), slice(Q_TILE // 2, Q_TILE)):
                acc = mlp["acc"][rows] + jnp.dot(mlp["z"][rows], wd,
                                                 preferred_element_type=F32)
                o_ref[rows, :] = acc * _rms_scale(acc) * g3_ref[...]

        units = []
        n_chunks = D_FF // FF_CHUNK
        for c in range(n_chunks):
            units += [functools.partial(up, c),
                      functools.partial(down if c + 1 < n_chunks else down_and_finish, c)]
        assert len(units) == N_Q_HEADS

        if attention:
            s_next = scores(0)
        if mlp_pass:
            mix()
        else:
            o_ref[...] = jnp.zeros_like(o_ref)
        halves = []
        for h in range(N_Q_HEADS):
            if attention:
                s = s_next
                if h + 1 < N_Q_HEADS:
                    s_next = scores(h + 1)
            if mlp_pass:
                units[h]()
            if not attention:
                continue
            m = jnp.max(s, axis=0, keepdims=True)
            e = jnp.exp2(s - m).astype(BF16)
            g = h // (N_Q_HEADS // N_KV_HEADS)
            r = jnp.dot(vxt_ref[0, g * V7X_LANES:(g + 1) * V7X_LANES, :], e,
                        preferred_element_type=F32)
            halves.append(r[:HEAD_DIM] / r[HEAD_DIM:])
            if h % 2 == 1:
                c = h // 2
                pair = jnp.concatenate(halves, axis=0)
                attn_ref[:, c * V7X_LANES:(c + 1) * V7X_LANES] = pair.T.astype(BF16)
                halves = []

    i = pl.program_id(0)
    last = pl.num_programs(0) - 1
    pl.when(i == 0)(functools.partial(step, True, False))
    pl.when(jnp.logical_and(i > 0, i < last))(functools.partial(step, True, True))
    pl.when(i == last)(functools.partial(step, False, True))


def _attn_mlp(qt, k, vxt, four, x2, w_out, g2, w_up, w_down, g3, batch, seq):
    n = x2.shape[0]
    tiles = n // Q_TILE
    per_seq = seq // Q_TILE
    attn_tile = lambda i: jnp.minimum(i, tiles - 1)
    mlp_tile = lambda i: jnp.maximum(i - 1, 0)
    return pl.pallas_call(
        _attn_mlp_kernel,
        grid=(tiles + 1,),
        in_specs=[
            pl.BlockSpec((1, ATTN_WIDTH, Q_TILE),
                         lambda i: (attn_tile(i) // per_seq, 0, attn_tile(i) % per_seq)),
            pl.BlockSpec((seq, KV_WIDTH), lambda i: (attn_tile(i) // per_seq, 0)),
            pl.BlockSpec((1, N_KV_HEADS * V7X_LANES, seq),
                         lambda i: (attn_tile(i) // per_seq, 0, 0)),
            pl.BlockSpec((Q_TILE, FOURIER_WIDTH), lambda i: (mlp_tile(i), 0)),
            pl.BlockSpec((Q_TILE, D_MODEL), lambda i: (mlp_tile(i), 0)),
            _resident((D_MODEL, D_MODEL)),
            _resident((1, D_MODEL)),
            _resident((D_MODEL, D_FF)),
            _resident((D_FF, D_MODEL)),
            _resident((1, D_MODEL)),
        ],
        out_specs=pl.BlockSpec((Q_TILE, D_MODEL), lambda i: (mlp_tile(i), 0)),
        out_shape=jax.ShapeDtypeStruct((n, D_MODEL), F32),
        scratch_shapes=[pltpu.VMEM((Q_TILE, ATTN_WIDTH), BF16)],
        compiler_params=pltpu.CompilerParams(
            dimension_semantics=("arbitrary",), vmem_limit_bytes=FUSED_VMEM_LIMIT),
        name="attn_mlp",
    )(qt, k, vxt, four, x2, w_out, g2, w_up, w_down, g3)


@functools.lru_cache(maxsize=None)
def _qk_layout():
    quarter = AXIS_DIM // 2
    firsts = np.concatenate([np.arange(quarter), AXIS_DIM + np.arange(quarter)])
    seconds = firsts + quarter
    q_cols = np.concatenate([HEAD_DIM * h + part for part in (firsts, seconds)
                             for h in range(N_Q_HEADS)])
    k_cols = ATTN_WIDTH + np.concatenate([HEAD_DIM * h + part for part in (firsts, seconds)
                                          for h in range(N_KV_HEADS)])
    old = np.concatenate([q_cols, k_cols])
    perm = np.zeros((old.size, old.size), np.float32)
    perm[old, np.arange(old.size)] = 1.0
    q_head = np.arange(V7X_LANES) // AXIS_DIM
    k_head = (np.arange(V7X_LANES) // AXIS_DIM) % N_KV_HEADS
    same = lambda head: (head[:, None] == head[None, :]).astype(np.float32)
    return perm, same(q_head), same(k_head)


def _lane_gains(g, copies_first, copies_second=0):
    quarter = AXIS_DIM // 2
    g = g.astype(F32)
    first = jnp.concatenate([g[:quarter], g[AXIS_DIM:AXIS_DIM + quarter]])
    second = jnp.concatenate([g[quarter:AXIS_DIM], g[AXIS_DIM + quarter:]])
    return jnp.concatenate([first] * copies_first + [second] * copies_second).reshape(1, V7X_LANES)


@functools.lru_cache(maxsize=None)
def _rope_tables(seq):
    rows = seq // GRID_W
    row = np.repeat(np.arange(rows), GRID_W).astype(np.float64)
    col = np.tile(np.arange(GRID_W), rows).astype(np.float64)
    inv_freq = ROPE_THETA ** (-np.arange(0, AXIS_DIM, 2, dtype=np.float64) / AXIS_DIM)
    ang = np.concatenate([row[:, None] * inv_freq[None, :], col[:, None] * inv_freq[None, :]],
                         axis=-1)
    ang = np.tile(ang, (1, V7X_LANES // AXIS_DIM))
    sin = np.sin(ang)
    k_first = np.arange(V7X_LANES) < HEAD_DIM
    f32 = lambda a: a.astype(np.float32)
    return f32(np.cos(ang)), f32(sin), f32(np.where(k_first, -sin, sin))


@functools.lru_cache(maxsize=None)
def _dft_constants(seq):
    assert seq == DFT_N1 * DFT_N2
    k1 = np.arange(DFT_N1, dtype=np.int64)
    a1 = 2.0 * np.pi * ((k1[:, None] * k1[None, :]) % DFT_N1) / DFT_N1
    m1 = np.concatenate([np.cos(a1), -np.sin(a1)], axis=0)
    n2 = np.arange(DFT_N2, dtype=np.int64)
    a2 = 2.0 * np.pi * ((n2[:, None] * n2[None, :]) % DFT_N2) / DFT_N2
    c2, s2 = np.cos(a2), np.sin(a2)
    r2 = np.block([[c2, s2], [-s2, c2]])
    at = 2.0 * np.pi * (k1[:, None] * n2[None, :]) / seq
    g2 = np.stack([r2 @ np.block([[np.diag(np.cos(a)), np.diag(np.sin(a))],
                                  [-np.diag(np.sin(a)), np.diag(np.cos(a))]]) for a in at])
    c = np.arange(GROUP_DIM, dtype=np.int64)
    angc = 2.0 * np.pi * ((c[:, None] * c[None, :]) % GROUP_DIM) / GROUP_DIM
    scale = 1.0 / np.sqrt(float(seq * GROUP_DIM))
    eye = np.eye(N_GROUPS)
    cc = np.kron(eye, np.cos(angc)) * scale
    sc = np.kron(eye, np.sin(angc)) * scale
    f32 = lambda a: a.astype(np.float32)
    return f32(m1), f32(g2), f32(cc), f32(sc)


def kernel(x, mix_norm_g, w_in, q_norm_g, k_norm_g, w_fourier, w_out, mlp_norm_g, w_up,
           w_down, final_norm_g):
    batch, seq, d_model = x.shape
    assert d_model == D_MODEL and seq % Q_TILE == 0 and seq % IN_TILE == 0
    x2 = x.reshape(batch * seq, d_model)

    m1_np, g2_np, cc_np, sc_np = _dft_constants(seq)
    m1 = jnp.asarray(m1_np).astype(BF16)
    g2 = jnp.asarray(g2_np).astype(BF16)
    cc, sc = jnp.asarray(cc_np), jnp.asarray(sc_np)
    cos_t, sin_t, sin_k = (jnp.asarray(a) for a in _rope_tables(seq))

    row = lambda g: g.reshape(1, -1).astype(F32)
    wbd = (jnp.eye(N_GROUPS, dtype=F32)[:, None, :, None]
           * w_fourier.astype(F32)[:, :, None, :]).reshape(FOURIER_WIDTH, FOURIER_WIDTH)

    perm, same_q, same_k = (jnp.asarray(a, dtype=BF16) for a in _qk_layout())
    per_block = V7X_LANES // AXIS_DIM
    qga = _lane_gains(q_norm_g, per_block)
    qgb = _lane_gains(q_norm_g, 0, per_block)
    kg = _lane_gains(k_norm_g, N_KV_HEADS, N_KV_HEADS)

    qt, k, vxt, u, w_out16, w_up16, w_down16 = _in_proj(
        x2, row(mix_norm_g), w_in.astype(F32), perm, same_q, same_k, qga, qgb, kg,
        cos_t, sin_t, sin_k, w_out.astype(F32), w_up.astype(F32), w_down.astype(F32),
        batch, seq)
    four = _fourier(u, m1, g2, cc, sc, wbd, batch, seq)
    out = _attn_mlp(qt, k, vxt, four, x2, w_out16, row(mlp_norm_g), w_up16, w_down16,
                    row(final_norm_g), batch, seq)
    return out.reshape(batch, seq, d_model)
```

```python
import functools

import numpy as np
import jax
import jax.numpy as jnp
from jax import lax
from jax.experimental import pallas as pl
from jax.experimental.pallas import tpu as pltpu

D_MODEL = 1024
HEAD_DIM = 64
N_Q_HEADS = 8
N_KV_HEADS = 2
ATTN_WIDTH = N_Q_HEADS * HEAD_DIM
KV_WIDTH = N_KV_HEADS * HEAD_DIM
N_GROUPS = 8
GROUP_DIM = 64
FOURIER_WIDTH = N_GROUPS * GROUP_DIM
IN_PROJ_WIDTH = ATTN_WIDTH + 2 * KV_WIDTH + FOURIER_WIDTH
D_FF = 4 * D_MODEL
GRID_W = 64
AXIS_DIM = HEAD_DIM // 2
ROPE_THETA = 10000.0
NORM_EPS = 1e-6
LOG2_E = 1.4426950408889634

V7X_LANES = 128
V7X_VMEM_BYTES = 64 * 1024 * 1024

IN_TILE = 2048
IN_SUB = 256
Q_TILE = 512
FF_CHUNK = 1024
DFT_N1 = 32
DFT_N2 = 64
DFT_K1_GROUP = 8
MIB = 1024 * 1024
VMEM_LIMIT = V7X_VMEM_BYTES - 8 * MIB
FUSED_VMEM_LIMIT = V7X_VMEM_BYTES - 4 * MIB

BF16 = jnp.bfloat16
F32 = jnp.float32


def _resident(shape):
    zeros = (0,) * len(shape)
    return pl.BlockSpec(shape, lambda *_: zeros, pipeline_mode=pl.Buffered(1))


def _rms_scale(x):
    return lax.rsqrt(jnp.mean(x * x, axis=-1, keepdims=True) + NORM_EPS)


def _in_proj_kernel(x_ref, g_ref, w_ref, ones_ref, qg_ref, kg_ref, cos_ref, sa_ref,
                    sb_ref, wo_ref, wu_ref, wd_ref,
                    qt_ref, k_ref, vxt_ref, u_ref, wo16_ref, wu16_ref, wd16_ref):
    wo16_ref[...] = wo_ref[...].astype(BF16)
    wu16_ref[...] = wu_ref[...].astype(BF16)
    wd16_ref[...] = wd_ref[...].astype(BF16)

    g = g_ref[...]
    ones = ones_ref[...]
    qg = qg_ref[...]
    kg = kg_ref[...]
    w = w_ref[...].astype(BF16)

    def project(i):
        x = x_ref[i * IN_SUB:(i + 1) * IN_SUB, :]
        h = x * _rms_scale(x) * g
        return jnp.dot(h.astype(BF16), w, preferred_element_type=F32)

    def finish(i, proj):
        rows = slice(i * IN_SUB, (i + 1) * IN_SUB)
        cos = cos_ref[rows, :]
        sin_a = sa_ref[rows, :]
        sin_b = sb_ref[rows, :]

        def norm_rope(blk, gain, scale):
            ss = jnp.dot((blk * blk).astype(BF16), ones, preferred_element_type=F32)
            y = blk * (lax.rsqrt(ss * (1.0 / HEAD_DIM) + NORM_EPS) * scale) * gain
            return (y * cos + pltpu.roll(y, V7X_LANES - 16, 1) * sin_a
                    + pltpu.roll(y, 16, 1) * sin_b)

        for c in range(ATTN_WIDTH // V7X_LANES):
            sl = slice(c * V7X_LANES, (c + 1) * V7X_LANES)
            qt_ref[0, sl, rows] = norm_rope(
                proj[:, sl], qg, HEAD_DIM ** -0.5 * LOG2_E).T.astype(BF16)

        k = norm_rope(proj[:, ATTN_WIDTH:ATTN_WIDTH + KV_WIDTH], kg, 1.0)
        k_ref[rows, :] = k.astype(BF16)

        vt = proj[:, ATTN_WIDTH + KV_WIDTH:ATTN_WIDTH + 2 * KV_WIDTH].T.astype(BF16)
        one = jnp.ones((HEAD_DIM, vt.shape[1]), BF16)
        for g in range(N_KV_HEADS):
            vxt_ref[0, g * V7X_LANES:g * V7X_LANES + HEAD_DIM, rows] = (
                vt[g * HEAD_DIM:(g + 1) * HEAD_DIM])
            vxt_ref[0, g * V7X_LANES + HEAD_DIM:(g + 1) * V7X_LANES, rows] = one

        u_ref[rows, :] = proj[:, ATTN_WIDTH + 2 * KV_WIDTH:].astype(BF16)

    n_sub = IN_TILE // IN_SUB
    nxt = project(0)
    for i in range(n_sub):
        cur = nxt
        if i + 1 < n_sub:
            nxt = project(i + 1)
        finish(i, cur)


def _in_proj(x2, mix_g, w_in, ones_bd, qg, kg, cos_t, sin_a, sin_b, w_out, w_up, w_down,
             batch, seq):
    n = x2.shape[0]
    steps = n // IN_TILE
    tiles_per_seq = seq // IN_TILE
    tok = lambda i: (i, 0)
    pos = lambda i: (i % tiles_per_seq, 0)
    chan_major = lambda i: (i // tiles_per_seq, 0, i % tiles_per_seq)
    slab = lambda w: pl.BlockSpec((w.shape[0] // steps, w.shape[1]), tok)
    bf16_like = lambda w: jax.ShapeDtypeStruct(w.shape, BF16)
    return pl.pallas_call(
        _in_proj_kernel,
        grid=(steps,),
        in_specs=[
            pl.BlockSpec((IN_TILE, D_MODEL), tok),
            _resident((1, D_MODEL)),
            _resident((D_MODEL, IN_PROJ_WIDTH)),
            _resident((V7X_LANES, V7X_LANES)),
            _resident((1, V7X_LANES)),
            _resident((1, V7X_LANES)),
            pl.BlockSpec((IN_TILE, V7X_LANES), pos),
            pl.BlockSpec((IN_TILE, V7X_LANES), pos),
            pl.BlockSpec((IN_TILE, V7X_LANES), pos),
            slab(w_out), slab(w_up), slab(w_down),
        ],
        out_specs=[
            pl.BlockSpec((1, ATTN_WIDTH, IN_TILE), chan_major),
            pl.BlockSpec((IN_TILE, KV_WIDTH), tok),
            pl.BlockSpec((1, N_KV_HEADS * V7X_LANES, IN_TILE), chan_major),
            pl.BlockSpec((IN_TILE, FOURIER_WIDTH), tok),
            slab(w_out), slab(w_up), slab(w_down),
        ],
        out_shape=[
            jax.ShapeDtypeStruct((batch, ATTN_WIDTH, seq), BF16),
            jax.ShapeDtypeStruct((n, KV_WIDTH), BF16),
            jax.ShapeDtypeStruct((batch, N_KV_HEADS * V7X_LANES, seq), BF16),
            jax.ShapeDtypeStruct((n, FOURIER_WIDTH), BF16),
            bf16_like(w_out), bf16_like(w_up), bf16_like(w_down),
        ],
        compiler_params=pltpu.CompilerParams(
            dimension_semantics=("arbitrary",), vmem_limit_bytes=VMEM_LIMIT),
        name="in_proj",
    )(x2, mix_g, w_in, ones_bd, qg, kg, cos_t, sin_a, sin_b, w_out, w_up, w_down)


def _fourier_kernel(u_ref, m1_ref, g2_ref, cc_ref, sc_ref, wbd_ref, y_ref, ma_ref, mb_ref):
    @pl.when(pl.program_id(0) == 0)
    def _():
        w = wbd_ref[...]
        ma_ref[...] = jnp.dot(cc_ref[...], w, precision=lax.Precision.HIGHEST,
                              preferred_element_type=F32).astype(BF16)
        mb_ref[...] = jnp.dot(sc_ref[...], w, precision=lax.Precision.HIGHEST,
                              preferred_element_type=F32).astype(BF16)

    x = u_ref[...].reshape(DFT_N1, DFT_N2, FOURIER_WIDTH)
    xt = jnp.swapaxes(x, 0, 1)

    m1 = m1_ref[...]
    q = []
    for n2 in range(DFT_N2):
        q.append(jnp.dot(m1, xt[n2], preferred_element_type=F32).astype(BF16))
    q = jnp.swapaxes(jnp.stack(q, axis=0), 0, 1)

    half = FOURIER_WIDTH // 2
    y = []
    for g in range(DFT_N1 // DFT_K1_GROUP):
        xr, xi = [], []
        for k1 in range(g * DFT_K1_GROUP, (g + 1) * DFT_K1_GROUP):
            qk = jnp.concatenate([q[k1], q[DFT_N1 + k1]], axis=0)
            xk = jnp.dot(g2_ref[k1], qk, preferred_element_type=F32)
            xr.append(xk[:DFT_N2].astype(BF16))
            xi.append(xk[DFT_N2:].astype(BF16))
        xr = jnp.concatenate(xr, axis=0)
        xi = jnp.concatenate(xi, axis=0)
        y.append(jnp.concatenate(
            [jnp.dot(xr[:, lanes], ma_ref[lanes, lanes], preferred_element_type=F32)
             + jnp.dot(xi[:, lanes], mb_ref[lanes, lanes], preferred_element_type=F32)
             for lanes in (slice(0, half), slice(half, FOURIER_WIDTH))], axis=1))
    y = jnp.concatenate(y, axis=0).astype(BF16).reshape(DFT_N1, DFT_N2, FOURIER_WIDTH)
    y_ref[...] = jnp.swapaxes(y, 0, 1).reshape(DFT_N1 * DFT_N2, FOURIER_WIDTH)


def _fourier(u, m1, g2, cc, sc, wbd, batch, seq):
    return pl.pallas_call(
        _fourier_kernel,
        grid=(batch,),
        in_specs=[
            pl.BlockSpec((seq, FOURIER_WIDTH), lambda b: (b, 0)),
            _resident((2 * DFT_N1, DFT_N1)),
            _resident((DFT_N1, 2 * DFT_N2, 2 * DFT_N2)),
            _resident((FOURIER_WIDTH, FOURIER_WIDTH)),
            _resident((FOURIER_WIDTH, FOURIER_WIDTH)),
            _resident((FOURIER_WIDTH, FOURIER_WIDTH)),
        ],
        out_specs=pl.BlockSpec((seq, FOURIER_WIDTH), lambda b: (b, 0)),
        out_shape=jax.ShapeDtypeStruct((batch * seq, FOURIER_WIDTH), BF16),
        scratch_shapes=[pltpu.VMEM((FOURIER_WIDTH, FOURIER_WIDTH), BF16),
                        pltpu.VMEM((FOURIER_WIDTH, FOURIER_WIDTH), BF16)],
        compiler_params=pltpu.CompilerParams(
            dimension_semantics=("arbitrary",), vmem_limit_bytes=VMEM_LIMIT),
        name="fourier",
    )(u, m1, g2, cc, sc, wbd)


def _attn_mlp_kernel(qt_ref, k_ref, vxt_ref, f_ref, x_ref, wo_ref, g2_ref, wu_ref, wd_ref,
                     g3_ref, o_ref, attn_ref):
    @pl.when(pl.program_id(0) == 0)
    def _():
        attn_ref[...] = jnp.zeros_like(attn_ref)

    def scores(h):
        q = qt_ref[0, h * HEAD_DIM:(h + 1) * HEAD_DIM, :]
        zero = jnp.zeros_like(q)
        g = h // (N_Q_HEADS // N_KV_HEADS)
        q_pad = jnp.concatenate([zero] * g + [q] + [zero] * (N_KV_HEADS - 1 - g), axis=0)
        return jnp.dot(k_ref[...], q_pad, preferred_element_type=F32)

    mlp = {}

    def mix():
        m = (jnp.dot(attn_ref[...], wo_ref[:ATTN_WIDTH, :], preferred_element_type=F32)
             + jnp.dot(f_ref[...], wo_ref[ATTN_WIDTH:, :], preferred_element_type=F32))
        x1 = x_ref[...] + m
        mlp["acc"] = x1
        mlp["h"] = (x1 * _rms_scale(x1) * g2_ref[...]).astype(BF16)

    def up(c):
        z = jnp.dot(mlp["h"], wu_ref[:, c * FF_CHUNK:(c + 1) * FF_CHUNK],
                    preferred_element_type=F32)
        mlp["z"] = jnp.square(jnp.maximum(z, 0.0)).astype(BF16)

    def down(c):
        mlp["acc"] = mlp["acc"] + jnp.dot(mlp["z"], wd_ref[c * FF_CHUNK:(c + 1) * FF_CHUNK, :],
                                         preferred_element_type=F32)

    def down_and_finish(c):
        wd = wd_ref[c * FF_CHUNK:(c + 1) * FF_CHUNK, :]
        for rows in (slice(0, Q_TILE // 2), slice(Q_TILE // 2, Q_TILE)):
            acc = mlp["acc"][rows] + jnp.dot(mlp["z"][rows], wd, preferred_element_type=F32)
            o_ref[rows, :] = acc * _rms_scale(acc) * g3_ref[...]

    units = []
    n_chunks = D_FF // FF_CHUNK
    for c in range(n_chunks):
        units += [functools.partial(up, c),
                  functools.partial(down if c + 1 < n_chunks else down_and_finish, c)]
    assert len(units) == N_Q_HEADS

    s_next = scores(0)
    mix()
    halves = []
    for h in range(N_Q_HEADS):
        s = s_next
        if h + 1 < N_Q_HEADS:
            s_next = scores(h + 1)
        units[h]()
        m = jnp.max(s, axis=0, keepdims=True)
        e = jnp.exp2(s - m).astype(BF16)
        g = h // (N_Q_HEADS // N_KV_HEADS)
        r = jnp.dot(vxt_ref[0, g * V7X_LANES:(g + 1) * V7X_LANES, :], e,
                    preferred_element_type=F32)
        halves.append(r[:HEAD_DIM] / r[HEAD_DIM:])
        if h % 2 == 1:
            c = h // 2
            pair = jnp.concatenate(halves, axis=0)
            attn_ref[:, c * V7X_LANES:(c + 1) * V7X_LANES] = pair.T.astype(BF16)
            halves = []


def _attn_mlp(qt, k, vxt, four, x2, w_out, g2, w_up, w_down, g3, batch, seq):
    n = x2.shape[0]
    tiles = n // Q_TILE
    per_seq = seq // Q_TILE
    attn_tile = lambda i: jnp.minimum(i, tiles - 1)
    mlp_tile = lambda i: jnp.maximum(i - 1, 0)
    return pl.pallas_call(
        _attn_mlp_kernel,
        grid=(tiles + 1,),
        in_specs=[
            pl.BlockSpec((1, ATTN_WIDTH, Q_TILE),
                         lambda i: (attn_tile(i) // per_seq, 0, attn_tile(i) % per_seq)),
            pl.BlockSpec((seq, KV_WIDTH), lambda i: (attn_tile(i) // per_seq, 0)),
            pl.BlockSpec((1, N_KV_HEADS * V7X_LANES, seq),
                         lambda i: (attn_tile(i) // per_seq, 0, 0)),
            pl.BlockSpec((Q_TILE, FOURIER_WIDTH), lambda i: (mlp_tile(i), 0)),
            pl.BlockSpec((Q_TILE, D_MODEL), lambda i: (mlp_tile(i), 0)),
            _resident((D_MODEL, D_MODEL)),
            _resident((1, D_MODEL)),
            _resident((D_MODEL, D_FF)),
            _resident((D_FF, D_MODEL)),
            _resident((1, D_MODEL)),
        ],
        out_specs=pl.BlockSpec((Q_TILE, D_MODEL), lambda i: (mlp_tile(i), 0)),
        out_shape=jax.ShapeDtypeStruct((n, D_MODEL), F32),
        scratch_shapes=[pltpu.VMEM((Q_TILE, ATTN_WIDTH), BF16)],
        compiler_params=pltpu.CompilerParams(
            dimension_semantics=("arbitrary",), vmem_limit_bytes=FUSED_VMEM_LIMIT),
        name="attn_mlp",
    )(qt, k, vxt, four, x2, w_out, g2, w_up, w_down, g3)


@functools.lru_cache(maxsize=None)
def _rope_tables(seq):
    rows = seq // GRID_W
    row = np.repeat(np.arange(rows), GRID_W).astype(np.float64)
    col = np.tile(np.arange(GRID_W), rows).astype(np.float64)
    inv_freq = ROPE_THETA ** (-np.arange(0, AXIS_DIM, 2, dtype=np.float64) / AXIS_DIM)
    row_ang = row[:, None] * inv_freq[None, :]
    col_ang = col[:, None] * inv_freq[None, :]
    ang = np.concatenate([row_ang, row_ang, col_ang, col_ang], axis=-1)
    ang = np.concatenate([ang, ang], axis=-1)
    first_half = (np.arange(V7X_LANES) % AXIS_DIM) < (AXIS_DIM // 2)
    sin = np.sin(ang)
    f32 = lambda a: a.astype(np.float32)
    return (f32(np.cos(ang)), f32(np.where(first_half, -sin, 0.0)),
            f32(np.where(first_half, 0.0, sin)))


@functools.lru_cache(maxsize=None)
def _dft_constants(seq):
    assert seq == DFT_N1 * DFT_N2
    k1 = np.arange(DFT_N1, dtype=np.int64)
    a1 = 2.0 * np.pi * ((k1[:, None] * k1[None, :]) % DFT_N1) / DFT_N1
    m1 = np.concatenate([np.cos(a1), -np.sin(a1)], axis=0)
    n2 = np.arange(DFT_N2, dtype=np.int64)
    a2 = 2.0 * np.pi * ((n2[:, None] * n2[None, :]) % DFT_N2) / DFT_N2
    c2, s2 = np.cos(a2), np.sin(a2)
    r2 = np.block([[c2, s2], [-s2, c2]])
    at = 2.0 * np.pi * (k1[:, None] * n2[None, :]) / seq
    g2 = np.stack([r2 @ np.block([[np.diag(np.cos(a)), np.diag(np.sin(a))],
                                  [-np.diag(np.sin(a)), np.diag(np.cos(a))]]) for a in at])
    c = np.arange(GROUP_DIM, dtype=np.int64)
    angc = 2.0 * np.pi * ((c[:, None] * c[None, :]) % GROUP_DIM) / GROUP_DIM
    scale = 1.0 / np.sqrt(float(seq * GROUP_DIM))
    eye = np.eye(N_GROUPS)
    cc = np.kron(eye, np.cos(angc)) * scale
    sc = np.kron(eye, np.sin(angc)) * scale
    ones_bd = np.kron(np.eye(V7X_LANES // HEAD_DIM), np.ones((HEAD_DIM, HEAD_DIM)))
    f32 = lambda a: a.astype(np.float32)
    return f32(m1), f32(g2), f32(cc), f32(sc), f32(ones_bd)


def kernel(x, mix_norm_g, w_in, q_norm_g, k_norm_g, w_fourier, w_out, mlp_norm_g, w_up,
           w_down, final_norm_g):
    batch, seq, d_model = x.shape
    assert d_model == D_MODEL and seq % Q_TILE == 0 and seq % IN_TILE == 0
    x2 = x.reshape(batch * seq, d_model)

    m1_np, g2_np, cc_np, sc_np, ones_np = _dft_constants(seq)
    m1 = jnp.asarray(m1_np).astype(BF16)
    g2 = jnp.asarray(g2_np).astype(BF16)
    cc, sc = jnp.asarray(cc_np), jnp.asarray(sc_np)
    ones_bd = jnp.asarray(ones_np, dtype=BF16)
    cos_t, sin_a, sin_b = (jnp.asarray(a) for a in _rope_tables(seq))

    row = lambda g: g.reshape(1, -1).astype(F32)
    two_heads = lambda g: jnp.tile(g.astype(F32), V7X_LANES // HEAD_DIM).reshape(1, V7X_LANES)
    wbd = (jnp.eye(N_GROUPS, dtype=F32)[:, None, :, None]
           * w_fourier.astype(F32)[:, :, None, :]).reshape(FOURIER_WIDTH, FOURIER_WIDTH)

    qt, k, vxt, u, w_out16, w_up16, w_down16 = _in_proj(
        x2, row(mix_norm_g), w_in.astype(F32), ones_bd, two_heads(q_norm_g),
        two_heads(k_norm_g), cos_t, sin_a, sin_b, w_out.astype(F32), w_up.astype(F32),
        w_down.astype(F32), batch, seq)
    four = _fourier(u, m1, g2, cc, sc, wbd, batch, seq)
    out = _attn_mlp(qt, k, vxt, four, x2, w_out16, row(mlp_norm_g), w_up16, w_down16,
                    row(final_norm_g), batch, seq)
    return out.reshape(batch, seq, d_model)
```

```python
import functools

import numpy as np
import jax
import jax.numpy as jnp
from jax import lax
from jax.experimental import pallas as pl
from jax.experimental.pallas import tpu as pltpu

D_MODEL = 1024
HEAD_DIM = 64
N_Q_HEADS = 8
N_KV_HEADS = 2
ATTN_WIDTH = N_Q_HEADS * HEAD_DIM
KV_WIDTH = N_KV_HEADS * HEAD_DIM
N_GROUPS = 8
GROUP_DIM = 64
FOURIER_WIDTH = N_GROUPS * GROUP_DIM
IN_PROJ_WIDTH = ATTN_WIDTH + 2 * KV_WIDTH + FOURIER_WIDTH
D_FF = 4 * D_MODEL
GRID_W = 64
AXIS_DIM = HEAD_DIM // 2
ROPE_THETA = 10000.0
NORM_EPS = 1e-6
LOG2_E = 1.4426950408889634

V7X_LANES = 128
V7X_VMEM_BYTES = 64 * 1024 * 1024

IN_TILE = 2048
IN_SUB = 256
Q_TILE = 512
FF_CHUNK = 1024
DFT_N1 = 32
DFT_N2 = 64
DFT_K1_GROUP = 8
VMEM_LIMIT = 56 * 1024 * 1024
FUSED_VMEM_LIMIT = 60 * 1024 * 1024

BF16 = jnp.bfloat16
F32 = jnp.float32


def _resident(shape):
    zeros = (0,) * len(shape)
    return pl.BlockSpec(shape, lambda *_: zeros, pipeline_mode=pl.Buffered(1))


def _rms_scale(x):
    return lax.rsqrt(jnp.mean(x * x, axis=-1, keepdims=True) + NORM_EPS)


def _in_proj_kernel(x_ref, g_ref, w_ref, ones_ref, qg_ref, kg_ref, cos_ref, sa_ref,
                    sb_ref, wo_ref, wu_ref, wd_ref,
                    qt_ref, k_ref, vxt_ref, u_ref, wo16_ref, wu16_ref, wd16_ref):
    wo16_ref[...] = wo_ref[...].astype(BF16)
    wu16_ref[...] = wu_ref[...].astype(BF16)
    wd16_ref[...] = wd_ref[...].astype(BF16)

    g = g_ref[...]
    ones = ones_ref[...]
    qg = qg_ref[...]
    kg = kg_ref[...]
    w = w_ref[...].astype(BF16)

    def project(i):
        x = x_ref[i * IN_SUB:(i + 1) * IN_SUB, :]
        h = x * _rms_scale(x) * g
        return jnp.dot(h.astype(BF16), w, preferred_element_type=F32)

    def finish(i, proj):
        rows = slice(i * IN_SUB, (i + 1) * IN_SUB)
        cos = cos_ref[rows, :]
        sin_a = sa_ref[rows, :]
        sin_b = sb_ref[rows, :]

        def norm_rope(blk, gain, scale):
            ss = jnp.dot((blk * blk).astype(BF16), ones, preferred_element_type=F32)
            y = blk * (lax.rsqrt(ss * (1.0 / HEAD_DIM) + NORM_EPS) * scale) * gain
            return (y * cos + pltpu.roll(y, V7X_LANES - 16, 1) * sin_a
                    + pltpu.roll(y, 16, 1) * sin_b)

        for c in range(ATTN_WIDTH // V7X_LANES):
            sl = slice(c * V7X_LANES, (c + 1) * V7X_LANES)
            qt_ref[0, sl, rows] = norm_rope(
                proj[:, sl], qg, HEAD_DIM ** -0.5 * LOG2_E).T.astype(BF16)

        k = norm_rope(proj[:, ATTN_WIDTH:ATTN_WIDTH + KV_WIDTH], kg, 1.0)
        k_ref[rows, :] = k.astype(BF16)

        vt = proj[:, ATTN_WIDTH + KV_WIDTH:ATTN_WIDTH + 2 * KV_WIDTH].T.astype(BF16)
        one = jnp.ones((HEAD_DIM, vt.shape[1]), BF16)
        for g in range(N_KV_HEADS):
            vxt_ref[0, g * V7X_LANES:g * V7X_LANES + HEAD_DIM, rows] = (
                vt[g * HEAD_DIM:(g + 1) * HEAD_DIM])
            vxt_ref[0, g * V7X_LANES + HEAD_DIM:(g + 1) * V7X_LANES, rows] = one

        u_ref[rows, :] = proj[:, ATTN_WIDTH + 2 * KV_WIDTH:].astype(BF16)

    n_sub = IN_TILE // IN_SUB
    nxt = project(0)
    for i in range(n_sub):
        cur = nxt
        if i + 1 < n_sub:
            nxt = project(i + 1)
        finish(i, cur)


def _in_proj(x2, mix_g, w_in, ones_bd, qg, kg, cos_t, sin_a, sin_b, w_out, w_up, w_down,
             batch, seq):
    n = x2.shape[0]
    steps = n // IN_TILE
    tiles_per_seq = seq // IN_TILE
    tok = lambda i: (i, 0)
    pos = lambda i: (i % tiles_per_seq, 0)
    chan_major = lambda i: (i // tiles_per_seq, 0, i % tiles_per_seq)
    slab = lambda w: pl.BlockSpec((w.shape[0] // steps, w.shape[1]), tok)
    bf16_like = lambda w: jax.ShapeDtypeStruct(w.shape, BF16)
    return pl.pallas_call(
        _in_proj_kernel,
        grid=(steps,),
        in_specs=[
            pl.BlockSpec((IN_TILE, D_MODEL), tok),
            _resident((1, D_MODEL)),
            _resident((D_MODEL, IN_PROJ_WIDTH)),
            _resident((V7X_LANES, V7X_LANES)),
            _resident((1, V7X_LANES)),
            _resident((1, V7X_LANES)),
            pl.BlockSpec((IN_TILE, V7X_LANES), pos),
            pl.BlockSpec((IN_TILE, V7X_LANES), pos),
            pl.BlockSpec((IN_TILE, V7X_LANES), pos),
            slab(w_out), slab(w_up), slab(w_down),
        ],
        out_specs=[
            pl.BlockSpec((1, ATTN_WIDTH, IN_TILE), chan_major),
            pl.BlockSpec((IN_TILE, KV_WIDTH), tok),
            pl.BlockSpec((1, N_KV_HEADS * V7X_LANES, IN_TILE), chan_major),
            pl.BlockSpec((IN_TILE, FOURIER_WIDTH), tok),
            slab(w_out), slab(w_up), slab(w_down),
        ],
        out_shape=[
            jax.ShapeDtypeStruct((batch, ATTN_WIDTH, seq), BF16),
            jax.ShapeDtypeStruct((n, KV_WIDTH), BF16),
            jax.ShapeDtypeStruct((batch, N_KV_HEADS * V7X_LANES, seq), BF16),
            jax.ShapeDtypeStruct((n, FOURIER_WIDTH), BF16),
            bf16_like(w_out), bf16_like(w_up), bf16_like(w_down),
        ],
        compiler_params=pltpu.CompilerParams(
            dimension_semantics=("arbitrary",), vmem_limit_bytes=VMEM_LIMIT),
        name="in_proj",
    )(x2, mix_g, w_in, ones_bd, qg, kg, cos_t, sin_a, sin_b, w_out, w_up, w_down)


def _fourier_kernel(u_ref, m1_ref, g2_ref, cc_ref, sc_ref, wbd_ref, y_ref, ma_ref, mb_ref):
    @pl.when(pl.program_id(0) == 0)
    def _():
        w = wbd_ref[...]
        ma_ref[...] = jnp.dot(cc_ref[...], w, precision=lax.Precision.HIGHEST,
                              preferred_element_type=F32).astype(BF16)
        mb_ref[...] = jnp.dot(sc_ref[...], w, precision=lax.Precision.HIGHEST,
                              preferred_element_type=F32).astype(BF16)

    x = u_ref[...].reshape(DFT_N1, DFT_N2, FOURIER_WIDTH)
    xt = jnp.swapaxes(x, 0, 1)

    m1 = m1_ref[...]
    q = []
    for n2 in range(DFT_N2):
        q.append(jnp.dot(m1, xt[n2], preferred_element_type=F32).astype(BF16))
    q = jnp.swapaxes(jnp.stack(q, axis=0), 0, 1)
    nyquist = DFT_N1 // 2
    no_im = jnp.zeros((DFT_N2, FOURIER_WIDTH), BF16)

    half = FOURIER_WIDTH // 2
    y = []
    for g in range(DFT_N1 // DFT_K1_GROUP):
        xr, xi = [], []
        for k1 in range(g * DFT_K1_GROUP, (g + 1) * DFT_K1_GROUP):
            src = min(k1, DFT_N1 - k1)
            im = q[nyquist + src] if src % nyquist else no_im
            qk = jnp.concatenate([q[src], im], axis=0)
            xk = jnp.dot(g2_ref[k1], qk, preferred_element_type=F32)
            xr.append(xk[:DFT_N2].astype(BF16))
            xi.append(xk[DFT_N2:].astype(BF16))
        xr = jnp.concatenate(xr, axis=0)
        xi = jnp.concatenate(xi, axis=0)
        y.append(jnp.concatenate(
            [jnp.dot(xr[:, lanes], ma_ref[lanes, lanes], preferred_element_type=F32)
             + jnp.dot(xi[:, lanes], mb_ref[lanes, lanes], preferred_element_type=F32)
             for lanes in (slice(0, half), slice(half, FOURIER_WIDTH))], axis=1))
    y = jnp.concatenate(y, axis=0).astype(BF16).reshape(DFT_N1, DFT_N2, FOURIER_WIDTH)
    y_ref[...] = jnp.swapaxes(y, 0, 1).reshape(DFT_N1 * DFT_N2, FOURIER_WIDTH)


def _fourier(u, m1, g2, cc, sc, wbd, batch, seq):
    return pl.pallas_call(
        _fourier_kernel,
        grid=(batch,),
        in_specs=[
            pl.BlockSpec((seq, FOURIER_WIDTH), lambda b: (b, 0)),
            _resident((DFT_N1, DFT_N1)),
            _resident((DFT_N1, 2 * DFT_N2, 2 * DFT_N2)),
            _resident((FOURIER_WIDTH, FOURIER_WIDTH)),
            _resident((FOURIER_WIDTH, FOURIER_WIDTH)),
            _resident((FOURIER_WIDTH, FOURIER_WIDTH)),
        ],
        out_specs=pl.BlockSpec((seq, FOURIER_WIDTH), lambda b: (b, 0)),
        out_shape=jax.ShapeDtypeStruct((batch * seq, FOURIER_WIDTH), BF16),
        scratch_shapes=[pltpu.VMEM((FOURIER_WIDTH, FOURIER_WIDTH), BF16),
                        pltpu.VMEM((FOURIER_WIDTH, FOURIER_WIDTH), BF16)],
        compiler_params=pltpu.CompilerParams(
            dimension_semantics=("arbitrary",), vmem_limit_bytes=VMEM_LIMIT),
        name="fourier",
    )(u, m1, g2, cc, sc, wbd)


def _attn_mlp_kernel(qt_ref, k_ref, vxt_ref, f_ref, x_ref, wo_ref, g2_ref, wu_ref, wd_ref,
                     g3_ref, o_ref, attn_ref):
    @pl.when(pl.program_id(0) == 0)
    def _():
        attn_ref[...] = jnp.zeros_like(attn_ref)

    def scores(h):
        q = qt_ref[0, h * HEAD_DIM:(h + 1) * HEAD_DIM, :]
        zero = jnp.zeros_like(q)
        g = h // (N_Q_HEADS // N_KV_HEADS)
        q_pad = jnp.concatenate([zero] * g + [q] + [zero] * (N_KV_HEADS - 1 - g), axis=0)
        return jnp.dot(k_ref[...], q_pad, preferred_element_type=F32)

    mlp = {}

    def mix():
        m = (jnp.dot(attn_ref[...], wo_ref[:ATTN_WIDTH, :], preferred_element_type=F32)
             + jnp.dot(f_ref[...], wo_ref[ATTN_WIDTH:, :], preferred_element_type=F32))
        x1 = x_ref[...] + m
        mlp["acc"] = x1
        mlp["h"] = (x1 * _rms_scale(x1) * g2_ref[...]).astype(BF16)

    def up(c):
        z = jnp.dot(mlp["h"], wu_ref[:, c * FF_CHUNK:(c + 1) * FF_CHUNK],
                    preferred_element_type=F32)
        mlp["z"] = jnp.square(jnp.maximum(z, 0.0)).astype(BF16)

    def down(c):
        mlp["acc"] = mlp["acc"] + jnp.dot(mlp["z"], wd_ref[c * FF_CHUNK:(c + 1) * FF_CHUNK, :],
                                         preferred_element_type=F32)

    def down_and_finish(c):
        wd = wd_ref[c * FF_CHUNK:(c + 1) * FF_CHUNK, :]
        for rows in (slice(0, Q_TILE // 2), slice(Q_TILE // 2, Q_TILE)):
            acc = mlp["acc"][rows] + jnp.dot(mlp["z"][rows], wd, preferred_element_type=F32)
            o_ref[rows, :] = acc * _rms_scale(acc) * g3_ref[...]

    units = []
    n_chunks = D_FF // FF_CHUNK
    for c in range(n_chunks):
        units += [functools.partial(up, c),
                  functools.partial(down if c + 1 < n_chunks else down_and_finish, c)]
    assert len(units) == N_Q_HEADS

    s_next = scores(0)
    mix()
    halves = []
    for h in range(N_Q_HEADS):
        s = s_next
        if h + 1 < N_Q_HEADS:
            s_next = scores(h + 1)
        units[h]()
        m = jnp.max(s, axis=0, keepdims=True)
        e = jnp.exp2(s - m).astype(BF16)
        g = h // (N_Q_HEADS // N_KV_HEADS)
        r = jnp.dot(vxt_ref[0, g * V7X_LANES:(g + 1) * V7X_LANES, :], e,
                    preferred_element_type=F32)
        halves.append(r[:HEAD_DIM] / r[HEAD_DIM:])
        if h % 2 == 1:
            c = h // 2
            pair = jnp.concatenate(halves, axis=0)
            attn_ref[:, c * V7X_LANES:(c + 1) * V7X_LANES] = pair.T.astype(BF16)
            halves = []


def _attn_mlp(qt, kx, vxt, four, x2, w_out, g2, w_up, w_down, g3, batch, seq):
    n = x2.shape[0]
    tiles = n // Q_TILE
    per_seq = seq // Q_TILE
    attn_tile = lambda i: jnp.minimum(i, tiles - 1)
    mlp_tile = lambda i: jnp.maximum(i - 1, 0)
    return pl.pallas_call(
        _attn_mlp_kernel,
        grid=(tiles + 1,),
        in_specs=[
            pl.BlockSpec((1, ATTN_WIDTH, Q_TILE),
                         lambda i: (attn_tile(i) // per_seq, 0, attn_tile(i) % per_seq)),
            pl.BlockSpec((seq, KV_WIDTH), lambda i: (attn_tile(i) // per_seq, 0)),
            pl.BlockSpec((1, N_KV_HEADS * V7X_LANES, seq),
                         lambda i: (attn_tile(i) // per_seq, 0, 0)),
            pl.BlockSpec((Q_TILE, FOURIER_WIDTH), lambda i: (mlp_tile(i), 0)),
            pl.BlockSpec((Q_TILE, D_MODEL), lambda i: (mlp_tile(i), 0)),
            _resident((D_MODEL, D_MODEL)),
            _resident((1, D_MODEL)),
            _resident((D_MODEL, D_FF)),
            _resident((D_FF, D_MODEL)),
            _resident((1, D_MODEL)),
        ],
        out_specs=pl.BlockSpec((Q_TILE, D_MODEL), lambda i: (mlp_tile(i), 0)),
        out_shape=jax.ShapeDtypeStruct((n, D_MODEL), F32),
        scratch_shapes=[pltpu.VMEM((Q_TILE, ATTN_WIDTH), BF16)],
        compiler_params=pltpu.CompilerParams(
            dimension_semantics=("arbitrary",), vmem_limit_bytes=FUSED_VMEM_LIMIT),
        name="attn_mlp",
    )(qt, kx, vxt, four, x2, w_out, g2, w_up, w_down, g3)


@functools.lru_cache(maxsize=None)
def _rope_tables(seq):
    rows = seq // GRID_W
    row = np.repeat(np.arange(rows), GRID_W).astype(np.float64)
    col = np.tile(np.arange(GRID_W), rows).astype(np.float64)
    inv_freq = ROPE_THETA ** (-np.arange(0, AXIS_DIM, 2, dtype=np.float64) / AXIS_DIM)
    row_ang = row[:, None] * inv_freq[None, :]
    col_ang = col[:, None] * inv_freq[None, :]
    ang = np.concatenate([row_ang, row_ang, col_ang, col_ang], axis=-1)
    ang = np.concatenate([ang, ang], axis=-1)
    first_half = (np.arange(V7X_LANES) % AXIS_DIM) < (AXIS_DIM // 2)
    sin = np.sin(ang)
    f32 = lambda a: a.astype(np.float32)
    return (f32(np.cos(ang)), f32(np.where(first_half, -sin, 0.0)),
            f32(np.where(first_half, 0.0, sin)))


@functools.lru_cache(maxsize=None)
def _dft_constants(seq):
    assert seq == DFT_N1 * DFT_N2
    k1 = np.arange(DFT_N1, dtype=np.int64)
    a1 = 2.0 * np.pi * ((k1[:, None] * k1[None, :]) % DFT_N1) / DFT_N1
    nyq = DFT_N1 // 2
    m1 = np.concatenate([np.cos(a1[:nyq + 1]), -np.sin(a1[1:nyq])], axis=0)
    n2 = np.arange(DFT_N2, dtype=np.int64)
    a2 = 2.0 * np.pi * ((n2[:, None] * n2[None, :]) % DFT_N2) / DFT_N2
    c2, s2 = np.cos(a2), np.sin(a2)
    r2 = np.block([[c2, s2], [-s2, c2]])
    at = 2.0 * np.pi * (k1[:, None] * n2[None, :]) / seq
    g2 = np.stack([r2 @ np.block([[np.diag(np.cos(a)), np.diag(np.sin(a))],
                                  [-np.diag(np.sin(a)), np.diag(np.cos(a))]]) for a in at])
    g2[nyq + 1:, :, DFT_N2:] *= -1.0
    c = np.arange(GROUP_DIM, dtype=np.int64)
    angc = 2.0 * np.pi * ((c[:, None] * c[None, :]) % GROUP_DIM) / GROUP_DIM
    scale = 1.0 / np.sqrt(float(seq * GROUP_DIM))
    eye = np.eye(N_GROUPS)
    cc = np.kron(eye, np.cos(angc)) * scale
    sc = np.kron(eye, np.sin(angc)) * scale
    ones_bd = np.kron(np.eye(V7X_LANES // HEAD_DIM), np.ones((HEAD_DIM, HEAD_DIM)))
    f32 = lambda a: a.astype(np.float32)
    return f32(m1), f32(g2), f32(cc), f32(sc), f32(ones_bd)


def kernel(x, mix_norm_g, w_in, q_norm_g, k_norm_g, w_fourier, w_out, mlp_norm_g, w_up,
           w_down, final_norm_g):
    batch, seq, d_model = x.shape
    assert d_model == D_MODEL and seq % Q_TILE == 0 and seq % IN_TILE == 0
    x2 = x.reshape(batch * seq, d_model)

    m1_np, g2_np, cc_np, sc_np, ones_np = _dft_constants(seq)
    m1 = jnp.asarray(m1_np).astype(BF16)
    g2 = jnp.asarray(g2_np).astype(BF16)
    cc, sc = jnp.asarray(cc_np), jnp.asarray(sc_np)
    ones_bd = jnp.asarray(ones_np, dtype=BF16)
    cos_t, sin_a, sin_b = (jnp.asarray(a) for a in _rope_tables(seq))

    row = lambda g: g.reshape(1, -1).astype(F32)
    two_heads = lambda g: jnp.tile(g.astype(F32), V7X_LANES // HEAD_DIM).reshape(1, V7X_LANES)
    wbd = (jnp.eye(N_GROUPS, dtype=F32)[:, None, :, None]
           * w_fourier.astype(F32)[:, :, None, :]).reshape(FOURIER_WIDTH, FOURIER_WIDTH)

    qt, kx, vxt, u, w_out16, w_up16, w_down16 = _in_proj(
        x2, row(mix_norm_g), w_in.astype(F32), ones_bd, two_heads(q_norm_g),
        two_heads(k_norm_g), cos_t, sin_a, sin_b, w_out.astype(F32), w_up.astype(F32),
        w_down.astype(F32), batch, seq)
    four = _fourier(u, m1, g2, cc, sc, wbd, batch, seq)
    out = _attn_mlp(qt, kx, vxt, four, x2, w_out16, row(mlp_norm_g), w_up16, w_down16,
                    row(final_norm_g), batch, seq)
    return out.reshape(batch, seq, d_model)
```

```python
import functools

import numpy as np
import jax
import jax.numpy as jnp
from jax import lax
from jax.experimental import pallas as pl
from jax.experimental.pallas import tpu as pltpu

D_MODEL = 1024
HEAD_DIM = 64
N_Q_HEADS = 8
N_KV_HEADS = 2
ATTN_WIDTH = N_Q_HEADS * HEAD_DIM
KV_WIDTH = N_KV_HEADS * HEAD_DIM
N_GROUPS = 8
GROUP_DIM = 64
FOURIER_WIDTH = N_GROUPS * GROUP_DIM
IN_PROJ_WIDTH = ATTN_WIDTH + 2 * KV_WIDTH + FOURIER_WIDTH
D_FF = 4 * D_MODEL
GRID_W = 64
AXIS_DIM = HEAD_DIM // 2
ROPE_THETA = 10000.0
NORM_EPS = 1e-6
LOG2_E = 1.4426950408889634

V7X_LANES = 128
V7X_VMEM_BYTES = 64 * 1024 * 1024

IN_TILE = 2048
IN_SUB = 256
Q_TILE = 512
FF_CHUNK = 1024
DFT_N1 = 32
DFT_N2 = 64
DFT_K1_GROUP = 8
VMEM_LIMIT = 56 * 1024 * 1024
FUSED_VMEM_LIMIT = 60 * 1024 * 1024

BF16 = jnp.bfloat16
F32 = jnp.float32


def _resident(shape):
    zeros = (0,) * len(shape)
    return pl.BlockSpec(shape, lambda *_: zeros, pipeline_mode=pl.Buffered(1))


def _rms_scale(x):
    return lax.rsqrt(jnp.mean(x * x, axis=-1, keepdims=True) + NORM_EPS)


def _in_proj_kernel(x_ref, g_ref, w_ref, ones_ref, qg_ref, kg_ref, cos_ref, sa_ref,
                    sb_ref, wo_ref, wu_ref, wd_ref,
                    qt_ref, k_ref, vxt_ref, u_ref, wo16_ref, wu16_ref, wd16_ref):
    wo16_ref[...] = wo_ref[...].astype(BF16)
    wu16_ref[...] = wu_ref[...].astype(BF16)
    wd16_ref[...] = wd_ref[...].astype(BF16)

    g = g_ref[...]
    ones = ones_ref[...]
    qg = qg_ref[...]
    kg = kg_ref[...]
    w = w_ref[...].astype(BF16)

    def project(i):
        x = x_ref[i * IN_SUB:(i + 1) * IN_SUB, :]
        h = x * _rms_scale(x) * g
        return jnp.dot(h.astype(BF16), w, preferred_element_type=F32)

    def finish(i, proj):
        rows = slice(i * IN_SUB, (i + 1) * IN_SUB)
        cos = cos_ref[rows, :]
        sin_a = sa_ref[rows, :]
        sin_b = sb_ref[rows, :]

        def norm_rope(blk, gain, scale):
            ss = jnp.dot((blk * blk).astype(BF16), ones, preferred_element_type=F32)
            y = blk * (lax.rsqrt(ss * (1.0 / HEAD_DIM) + NORM_EPS) * scale) * gain
            return (y * cos + pltpu.roll(y, V7X_LANES - 16, 1) * sin_a
                    + pltpu.roll(y, 16, 1) * sin_b)

        for c in range(ATTN_WIDTH // V7X_LANES):
            sl = slice(c * V7X_LANES, (c + 1) * V7X_LANES)
            qt_ref[0, sl, rows] = norm_rope(
                proj[:, sl], qg, HEAD_DIM ** -0.5 * LOG2_E).T.astype(BF16)

        k = norm_rope(proj[:, ATTN_WIDTH:ATTN_WIDTH + KV_WIDTH], kg, 1.0)
        k_ref[rows, :] = k.astype(BF16)

        vt = proj[:, ATTN_WIDTH + KV_WIDTH:ATTN_WIDTH + 2 * KV_WIDTH].T.astype(BF16)
        one = jnp.ones((HEAD_DIM, vt.shape[1]), BF16)
        for g in range(N_KV_HEADS):
            vxt_ref[0, g * V7X_LANES:g * V7X_LANES + HEAD_DIM, rows] = (
                vt[g * HEAD_DIM:(g + 1) * HEAD_DIM])
            vxt_ref[0, g * V7X_LANES + HEAD_DIM:(g + 1) * V7X_LANES, rows] = one

        u_ref[rows, :] = proj[:, ATTN_WIDTH + 2 * KV_WIDTH:].astype(BF16)

    n_sub = IN_TILE // IN_SUB
    nxt = project(0)
    for i in range(n_sub):
        cur = nxt
        if i + 1 < n_sub:
            nxt = project(i + 1)
        finish(i, cur)


def _in_proj(x2, mix_g, w_in, ones_bd, qg, kg, cos_t, sin_a, sin_b, w_out, w_up, w_down,
             batch, seq):
    n = x2.shape[0]
    steps = n // IN_TILE
    tiles_per_seq = seq // IN_TILE
    tok = lambda i: (i, 0)
    pos = lambda i: (i % tiles_per_seq, 0)
    chan_major = lambda i: (i // tiles_per_seq, 0, i % tiles_per_seq)
    slab = lambda w: pl.BlockSpec((w.shape[0] // steps, w.shape[1]), tok)
    bf16_like = lambda w: jax.ShapeDtypeStruct(w.shape, BF16)
    return pl.pallas_call(
        _in_proj_kernel,
        grid=(steps,),
        in_specs=[
            pl.BlockSpec((IN_TILE, D_MODEL), tok),
            _resident((1, D_MODEL)),
            _resident((D_MODEL, IN_PROJ_WIDTH)),
            _resident((V7X_LANES, V7X_LANES)),
            _resident((1, V7X_LANES)),
            _resident((1, V7X_LANES)),
            pl.BlockSpec((IN_TILE, V7X_LANES), pos),
            pl.BlockSpec((IN_TILE, V7X_LANES), pos),
            pl.BlockSpec((IN_TILE, V7X_LANES), pos),
            slab(w_out), slab(w_up), slab(w_down),
        ],
        out_specs=[
            pl.BlockSpec((1, ATTN_WIDTH, IN_TILE), chan_major),
            pl.BlockSpec((IN_TILE, KV_WIDTH), tok),
            pl.BlockSpec((1, N_KV_HEADS * V7X_LANES, IN_TILE), chan_major),
            pl.BlockSpec((IN_TILE, FOURIER_WIDTH), tok),
            slab(w_out), slab(w_up), slab(w_down),
        ],
        out_shape=[
            jax.ShapeDtypeStruct((batch, ATTN_WIDTH, seq), BF16),
            jax.ShapeDtypeStruct((n, KV_WIDTH), BF16),
            jax.ShapeDtypeStruct((batch, N_KV_HEADS * V7X_LANES, seq), BF16),
            jax.ShapeDtypeStruct((n, FOURIER_WIDTH), BF16),
            bf16_like(w_out), bf16_like(w_up), bf16_like(w_down),
        ],
        compiler_params=pltpu.CompilerParams(
            dimension_semantics=("arbitrary",), vmem_limit_bytes=VMEM_LIMIT),
        name="in_proj",
    )(x2, mix_g, w_in, ones_bd, qg, kg, cos_t, sin_a, sin_b, w_out, w_up, w_down)


def _fourier_kernel(u_ref, m1_ref, g2_ref, cc_ref, sc_ref, wbd_ref, y_ref, ma_ref, mb_ref):
    @pl.when(pl.program_id(0) == 0)
    def _():
        w = wbd_ref[...]
        ma_ref[...] = jnp.dot(cc_ref[...], w, precision=lax.Precision.HIGHEST,
                              preferred_element_type=F32).astype(BF16)
        mb_ref[...] = jnp.dot(sc_ref[...], w, precision=lax.Precision.HIGHEST,
                              preferred_element_type=F32).astype(BF16)

    x = u_ref[...].reshape(DFT_N1, DFT_N2, FOURIER_WIDTH)
    xt = jnp.swapaxes(x, 0, 1)

    m1 = m1_ref[...]
    q = []
    for n2 in range(DFT_N2):
        q.append(jnp.dot(m1, xt[n2], preferred_element_type=F32).astype(BF16))
    q = jnp.swapaxes(jnp.stack(q, axis=0), 0, 1)
    nyquist = DFT_N1 // 2
    no_im = jnp.zeros((DFT_N2, FOURIER_WIDTH), BF16)

    half = FOURIER_WIDTH // 2
    lane_halves = (slice(0, half), slice(half, FOURIER_WIDTH))
    direct, mirrored = [], {}
    for start in range(0, nyquist + 1, DFT_K1_GROUP):
        group = range(start, min(start + DFT_K1_GROUP, nyquist + 1))
        xr, xi = [], []
        for k1 in group:
            im = q[nyquist + k1] if k1 % nyquist else no_im
            qk = jnp.concatenate([q[k1], im], axis=0)
            xk = jnp.dot(g2_ref[k1], qk, preferred_element_type=F32)
            xr.append(xk[:DFT_N2].astype(BF16))
            xi.append(xk[DFT_N2:].astype(BF16))
        xr = jnp.concatenate(xr, axis=0)
        xi = jnp.concatenate(xi, axis=0)
        p = jnp.concatenate([jnp.dot(xr[:, lanes], ma_ref[lanes, lanes],
                                     preferred_element_type=F32) for lanes in lane_halves], axis=1)
        m = jnp.concatenate([jnp.dot(xi[:, lanes], mb_ref[lanes, lanes],
                                     preferred_element_type=F32) for lanes in lane_halves], axis=1)
        plus, minus = (p + m).astype(BF16), (p - m).astype(BF16)
        for j, k1 in enumerate(group):
            rows = slice(j * DFT_N2, (j + 1) * DFT_N2)
            if k1 < nyquist:
                direct.append(plus[rows])
            if k1 > 0:
                mirrored[k1] = minus[rows]
    low = jnp.swapaxes(jnp.stack(direct, axis=0), 0, 1)
    high = jnp.swapaxes(jnp.stack([mirrored[k1] for k1 in range(nyquist, 0, -1)], axis=0),
                        0, 1)
    for k2 in range(DFT_N2):
        y_ref[k2 * DFT_N1:k2 * DFT_N1 + nyquist, :] = low[k2]
        y_ref[k2 * DFT_N1 + nyquist:(k2 + 1) * DFT_N1, :] = high[DFT_N2 - 1 - k2]


def _fourier(u, m1, g2, cc, sc, wbd, batch, seq):
    return pl.pallas_call(
        _fourier_kernel,
        grid=(batch,),
        in_specs=[
            pl.BlockSpec((seq, FOURIER_WIDTH), lambda b: (b, 0)),
            _resident((DFT_N1, DFT_N1)),
            _resident((DFT_N1 // 2 + 1, 2 * DFT_N2, 2 * DFT_N2)),
            _resident((FOURIER_WIDTH, FOURIER_WIDTH)),
            _resident((FOURIER_WIDTH, FOURIER_WIDTH)),
            _resident((FOURIER_WIDTH, FOURIER_WIDTH)),
        ],
        out_specs=pl.BlockSpec((seq, FOURIER_WIDTH), lambda b: (b, 0)),
        out_shape=jax.ShapeDtypeStruct((batch * seq, FOURIER_WIDTH), BF16),
        scratch_shapes=[pltpu.VMEM((FOURIER_WIDTH, FOURIER_WIDTH), BF16),
                        pltpu.VMEM((FOURIER_WIDTH, FOURIER_WIDTH), BF16)],
        compiler_params=pltpu.CompilerParams(
            dimension_semantics=("arbitrary",), vmem_limit_bytes=VMEM_LIMIT),
        name="fourier",
    )(u, m1, g2, cc, sc, wbd)


def _attn_mlp_kernel(qt_ref, k_ref, vxt_ref, f_ref, x_ref, wo_ref, g2_ref, wu_ref, wd_ref,
                     g3_ref, o_ref, attn_ref):
    @pl.when(pl.program_id(0) == 0)
    def _():
        attn_ref[...] = jnp.zeros_like(attn_ref)

    def scores(h):
        q = qt_ref[0, h * HEAD_DIM:(h + 1) * HEAD_DIM, :]
        zero = jnp.zeros_like(q)
        g = h // (N_Q_HEADS // N_KV_HEADS)
        q_pad = jnp.concatenate([zero] * g + [q] + [zero] * (N_KV_HEADS - 1 - g), axis=0)
        return jnp.dot(k_ref[...], q_pad, preferred_element_type=F32)

    mlp = {}

    def mix():
        m = (jnp.dot(attn_ref[...], wo_ref[:ATTN_WIDTH, :], preferred_element_type=F32)
             + jnp.dot(f_ref[...], wo_ref[ATTN_WIDTH:, :], preferred_element_type=F32))
        x1 = x_ref[...] + m
        mlp["acc"] = x1
        mlp["h"] = (x1 * _rms_scale(x1) * g2_ref[...]).astype(BF16)

    def up(c):
        z = jnp.dot(mlp["h"], wu_ref[:, c * FF_CHUNK:(c + 1) * FF_CHUNK],
                    preferred_element_type=F32)
        mlp["z"] = jnp.square(jnp.maximum(z, 0.0)).astype(BF16)

    def down(c):
        mlp["acc"] = mlp["acc"] + jnp.dot(mlp["z"], wd_ref[c * FF_CHUNK:(c + 1) * FF_CHUNK, :],
                                         preferred_element_type=F32)

    def down_and_finish(c):
        wd = wd_ref[c * FF_CHUNK:(c + 1) * FF_CHUNK, :]
        for rows in (slice(0, Q_TILE // 2), slice(Q_TILE // 2, Q_TILE)):
            acc = mlp["acc"][rows] + jnp.dot(mlp["z"][rows], wd, preferred_element_type=F32)
            o_ref[rows, :] = acc * _rms_scale(acc) * g3_ref[...]

    units = []
    n_chunks = D_FF // FF_CHUNK
    for c in range(n_chunks):
        units += [functools.partial(up, c),
                  functools.partial(down if c + 1 < n_chunks else down_and_finish, c)]
    assert len(units) == N_Q_HEADS

    s_next = scores(0)
    mix()
    halves = []
    for h in range(N_Q_HEADS):
        s = s_next
        if h + 1 < N_Q_HEADS:
            s_next = scores(h + 1)
        units[h]()
        m = jnp.max(s, axis=0, keepdims=True)
        e = jnp.exp2(s - m).astype(BF16)
        g = h // (N_Q_HEADS // N_KV_HEADS)
        r = jnp.dot(vxt_ref[0, g * V7X_LANES:(g + 1) * V7X_LANES, :], e,
                    preferred_element_type=F32)
        halves.append(r[:HEAD_DIM] / r[HEAD_DIM:])
        if h % 2 == 1:
            c = h // 2
            pair = jnp.concatenate(halves, axis=0)
            attn_ref[:, c * V7X_LANES:(c + 1) * V7X_LANES] = pair.T.astype(BF16)
            halves = []


def _attn_mlp(qt, kx, vxt, four, x2, w_out, g2, w_up, w_down, g3, batch, seq):
    n = x2.shape[0]
    tiles = n // Q_TILE
    per_seq = seq // Q_TILE
    attn_tile = lambda i: jnp.minimum(i, tiles - 1)
    mlp_tile = lambda i: jnp.maximum(i - 1, 0)
    return pl.pallas_call(
        _attn_mlp_kernel,
        grid=(tiles + 1,),
        in_specs=[
            pl.BlockSpec((1, ATTN_WIDTH, Q_TILE),
                         lambda i: (attn_tile(i) // per_seq, 0, attn_tile(i) % per_seq)),
            pl.BlockSpec((seq, KV_WIDTH), lambda i: (attn_tile(i) // per_seq, 0)),
            pl.BlockSpec((1, N_KV_HEADS * V7X_LANES, seq),
                         lambda i: (attn_tile(i) // per_seq, 0, 0)),
            pl.BlockSpec((Q_TILE, FOURIER_WIDTH), lambda i: (mlp_tile(i), 0)),
            pl.BlockSpec((Q_TILE, D_MODEL), lambda i: (mlp_tile(i), 0)),
            _resident((D_MODEL, D_MODEL)),
            _resident((1, D_MODEL)),
            _resident((D_MODEL, D_FF)),
            _resident((D_FF, D_MODEL)),
            _resident((1, D_MODEL)),
        ],
        out_specs=pl.BlockSpec((Q_TILE, D_MODEL), lambda i: (mlp_tile(i), 0)),
        out_shape=jax.ShapeDtypeStruct((n, D_MODEL), F32),
        scratch_shapes=[pltpu.VMEM((Q_TILE, ATTN_WIDTH), BF16)],
        compiler_params=pltpu.CompilerParams(
            dimension_semantics=("arbitrary",), vmem_limit_bytes=FUSED_VMEM_LIMIT),
        name="attn_mlp",
    )(qt, kx, vxt, four, x2, w_out, g2, w_up, w_down, g3)


@functools.lru_cache(maxsize=None)
def _rope_tables(seq):
    rows = seq // GRID_W
    row = np.repeat(np.arange(rows), GRID_W).astype(np.float64)
    col = np.tile(np.arange(GRID_W), rows).astype(np.float64)
    inv_freq = ROPE_THETA ** (-np.arange(0, AXIS_DIM, 2, dtype=np.float64) / AXIS_DIM)
    row_ang = row[:, None] * inv_freq[None, :]
    col_ang = col[:, None] * inv_freq[None, :]
    ang = np.concatenate([row_ang, row_ang, col_ang, col_ang], axis=-1)
    ang = np.concatenate([ang, ang], axis=-1)
    first_half = (np.arange(V7X_LANES) % AXIS_DIM) < (AXIS_DIM // 2)
    sin = np.sin(ang)
    f32 = lambda a: a.astype(np.float32)
    return (f32(np.cos(ang)), f32(np.where(first_half, -sin, 0.0)),
            f32(np.where(first_half, 0.0, sin)))


@functools.lru_cache(maxsize=None)
def _dft_constants(seq):
    assert seq == DFT_N1 * DFT_N2
    k1 = np.arange(DFT_N1, dtype=np.int64)
    a1 = 2.0 * np.pi * ((k1[:, None] * k1[None, :]) % DFT_N1) / DFT_N1
    nyq = DFT_N1 // 2
    m1 = np.concatenate([np.cos(a1[:nyq + 1]), -np.sin(a1[1:nyq])], axis=0)
    n2 = np.arange(DFT_N2, dtype=np.int64)
    a2 = 2.0 * np.pi * ((n2[:, None] * n2[None, :]) % DFT_N2) / DFT_N2
    c2, s2 = np.cos(a2), np.sin(a2)
    r2 = np.block([[c2, s2], [-s2, c2]])
    at = 2.0 * np.pi * (k1[:nyq + 1, None] * n2[None, :]) / seq
    g2 = np.stack([r2 @ np.block([[np.diag(np.cos(a)), np.diag(np.sin(a))],
                                  [-np.diag(np.sin(a)), np.diag(np.cos(a))]]) for a in at])
    c = np.arange(GROUP_DIM, dtype=np.int64)
    angc = 2.0 * np.pi * ((c[:, None] * c[None, :]) % GROUP_DIM) / GROUP_DIM
    scale = 1.0 / np.sqrt(float(seq * GROUP_DIM))
    eye = np.eye(N_GROUPS)
    cc = np.kron(eye, np.cos(angc)) * scale
    sc = np.kron(eye, np.sin(angc)) * scale
    ones_bd = np.kron(np.eye(V7X_LANES // HEAD_DIM), np.ones((HEAD_DIM, HEAD_DIM)))
    f32 = lambda a: a.astype(np.float32)
    return f32(m1), f32(g2), f32(cc), f32(sc), f32(ones_bd)


def kernel(x, mix_norm_g, w_in, q_norm_g, k_norm_g, w_fourier, w_out, mlp_norm_g, w_up,
           w_down, final_norm_g):
    batch, seq, d_model = x.shape
    assert d_model == D_MODEL and seq % Q_TILE == 0 and seq % IN_TILE == 0
    x2 = x.reshape(batch * seq, d_model)

    m1_np, g2_np, cc_np, sc_np, ones_np = _dft_constants(seq)
    m1 = jnp.asarray(m1_np).astype(BF16)
    g2 = jnp.asarray(g2_np).astype(BF16)
    cc, sc = jnp.asarray(cc_np), jnp.asarray(sc_np)
    ones_bd = jnp.asarray(ones_np, dtype=BF16)
    cos_t, sin_a, sin_b = (jnp.asarray(a) for a in _rope_tables(seq))

    row = lambda g: g.reshape(1, -1).astype(F32)
    two_heads = lambda g: jnp.tile(g.astype(F32), V7X_LANES // HEAD_DIM).reshape(1, V7X_LANES)
    wbd = (jnp.eye(N_GROUPS, dtype=F32)[:, None, :, None]
           * w_fourier.astype(F32)[:, :, None, :]).reshape(FOURIER_WIDTH, FOURIER_WIDTH)

    qt, kx, vxt, u, w_out16, w_up16, w_down16 = _in_proj(
        x2, row(mix_norm_g), w_in.astype(F32), ones_bd, two_heads(q_norm_g),
        two_heads(k_norm_g), cos_t, sin_a, sin_b, w_out.astype(F32), w_up.astype(F32),
        w_down.astype(F32), batch, seq)
    four = _fourier(u, m1, g2, cc, sc, wbd, batch, seq)
    out = _attn_mlp(qt, kx, vxt, four, x2, w_out16, row(mlp_norm_g), w_up16, w_down16,
                    row(final_norm_g), batch, seq)
    return out.reshape(batch, seq, d_model)
```

```python
import functools

import numpy as np
import jax
import jax.numpy as jnp
from jax import lax
from jax.experimental import pallas as pl
from jax.experimental.pallas import tpu as pltpu

D_MODEL = 1024
HEAD_DIM = 64
N_Q_HEADS = 8
N_KV_HEADS = 2
ATTN_WIDTH = N_Q_HEADS * HEAD_DIM
KV_WIDTH = N_KV_HEADS * HEAD_DIM
N_GROUPS = 8
GROUP_DIM = 64
FOURIER_WIDTH = N_GROUPS * GROUP_DIM
IN_PROJ_WIDTH = ATTN_WIDTH + 2 * KV_WIDTH + FOURIER_WIDTH
D_FF = 4 * D_MODEL
GRID_W = 64
AXIS_DIM = HEAD_DIM // 2
ROPE_THETA = 10000.0
NORM_EPS = 1e-6
LOG2_E = 1.4426950408889634

V7X_LANES = 128
V7X_VMEM_BYTES = 64 * 1024 * 1024

IN_TILE = 2048
IN_SUB = 256
Q_TILE = 512
FF_CHUNK = 1024
DFT_N1 = 32
DFT_N2 = 64
DFT_K1_GROUP = 9
VMEM_LIMIT = 56 * 1024 * 1024
FUSED_VMEM_LIMIT = 60 * 1024 * 1024

BF16 = jnp.bfloat16
F32 = jnp.float32


def _resident(shape):
    zeros = (0,) * len(shape)
    return pl.BlockSpec(shape, lambda *_: zeros, pipeline_mode=pl.Buffered(1))


def _rms_scale(x):
    return lax.rsqrt(jnp.mean(x * x, axis=-1, keepdims=True) + NORM_EPS)


def _in_proj_kernel(x_ref, g_ref, w_ref, ones_ref, qg_ref, kg_ref, cos_ref, sa_ref,
                    sb_ref, wo_ref, wu_ref, wd_ref,
                    qt_ref, k_ref, vxt_ref, u_ref, wo16_ref, wu16_ref, wd16_ref):
    wo16_ref[...] = wo_ref[...].astype(BF16)
    wu16_ref[...] = wu_ref[...].astype(BF16)
    wd16_ref[...] = wd_ref[...].astype(BF16)

    g = g_ref[...]
    ones = ones_ref[...]
    qg = qg_ref[...]
    kg = kg_ref[...]
    w = w_ref[...].astype(BF16)

    def project(i):
        x = x_ref[i * IN_SUB:(i + 1) * IN_SUB, :]
        h = x * _rms_scale(x) * g
        return jnp.dot(h.astype(BF16), w, preferred_element_type=F32)

    def finish(i, proj):
        rows = slice(i * IN_SUB, (i + 1) * IN_SUB)
        cos = cos_ref[rows, :]
        sin_a = sa_ref[rows, :]
        sin_b = sb_ref[rows, :]

        def norm_rope(blk, gain, scale):
            ss = jnp.dot((blk * blk).astype(BF16), ones, preferred_element_type=F32)
            y = blk * (lax.rsqrt(ss * (1.0 / HEAD_DIM) + NORM_EPS) * scale) * gain
            return (y * cos + pltpu.roll(y, V7X_LANES - 16, 1) * sin_a
                    + pltpu.roll(y, 16, 1) * sin_b)

        for c in range(ATTN_WIDTH // V7X_LANES):
            sl = slice(c * V7X_LANES, (c + 1) * V7X_LANES)
            qt_ref[0, sl, rows] = norm_rope(
                proj[:, sl], qg, HEAD_DIM ** -0.5 * LOG2_E).T.astype(BF16)

        k = norm_rope(proj[:, ATTN_WIDTH:ATTN_WIDTH + KV_WIDTH], kg, 1.0)
        k_ref[rows, :] = k.astype(BF16)

        vt = proj[:, ATTN_WIDTH + KV_WIDTH:ATTN_WIDTH + 2 * KV_WIDTH].T.astype(BF16)
        one = jnp.ones((HEAD_DIM, vt.shape[1]), BF16)
        for g in range(N_KV_HEADS):
            vxt_ref[0, g * V7X_LANES:g * V7X_LANES + HEAD_DIM, rows] = (
                vt[g * HEAD_DIM:(g + 1) * HEAD_DIM])
            vxt_ref[0, g * V7X_LANES + HEAD_DIM:(g + 1) * V7X_LANES, rows] = one

        u_ref[rows, :] = proj[:, ATTN_WIDTH + 2 * KV_WIDTH:].astype(BF16)

    n_sub = IN_TILE // IN_SUB
    nxt = project(0)
    for i in range(n_sub):
        cur = nxt
        if i + 1 < n_sub:
            nxt = project(i + 1)
        finish(i, cur)


def _in_proj(x2, mix_g, w_in, ones_bd, qg, kg, cos_t, sin_a, sin_b, w_out, w_up, w_down,
             batch, seq):
    n = x2.shape[0]
    steps = n // IN_TILE
    tiles_per_seq = seq // IN_TILE
    tok = lambda i: (i, 0)
    pos = lambda i: (i % tiles_per_seq, 0)
    chan_major = lambda i: (i // tiles_per_seq, 0, i % tiles_per_seq)
    slab = lambda w: pl.BlockSpec((w.shape[0] // steps, w.shape[1]), tok)
    bf16_like = lambda w: jax.ShapeDtypeStruct(w.shape, BF16)
    return pl.pallas_call(
        _in_proj_kernel,
        grid=(steps,),
        in_specs=[
            pl.BlockSpec((IN_TILE, D_MODEL), tok),
            _resident((1, D_MODEL)),
            _resident((D_MODEL, IN_PROJ_WIDTH)),
            _resident((V7X_LANES, V7X_LANES)),
            _resident((1, V7X_LANES)),
            _resident((1, V7X_LANES)),
            pl.BlockSpec((IN_TILE, V7X_LANES), pos),
            pl.BlockSpec((IN_TILE, V7X_LANES), pos),
            pl.BlockSpec((IN_TILE, V7X_LANES), pos),
            slab(w_out), slab(w_up), slab(w_down),
        ],
        out_specs=[
            pl.BlockSpec((1, ATTN_WIDTH, IN_TILE), chan_major),
            pl.BlockSpec((IN_TILE, KV_WIDTH), tok),
            pl.BlockSpec((1, N_KV_HEADS * V7X_LANES, IN_TILE), chan_major),
            pl.BlockSpec((IN_TILE, FOURIER_WIDTH), tok),
            slab(w_out), slab(w_up), slab(w_down),
        ],
        out_shape=[
            jax.ShapeDtypeStruct((batch, ATTN_WIDTH, seq), BF16),
            jax.ShapeDtypeStruct((n, KV_WIDTH), BF16),
            jax.ShapeDtypeStruct((batch, N_KV_HEADS * V7X_LANES, seq), BF16),
            jax.ShapeDtypeStruct((n, FOURIER_WIDTH), BF16),
            bf16_like(w_out), bf16_like(w_up), bf16_like(w_down),
        ],
        compiler_params=pltpu.CompilerParams(
            dimension_semantics=("arbitrary",), vmem_limit_bytes=VMEM_LIMIT),
        name="in_proj",
    )(x2, mix_g, w_in, ones_bd, qg, kg, cos_t, sin_a, sin_b, w_out, w_up, w_down)


def _fourier_kernel(u_ref, m1_ref, g2_ref, cc_ref, sc_ref, wbd_ref, y_ref, ma_ref, mb_ref):
    @pl.when(pl.program_id(0) == 0)
    def _():
        for lanes in (slice(0, FOURIER_WIDTH // 2), slice(FOURIER_WIDTH // 2, FOURIER_WIDTH)):
            w = wbd_ref[lanes, lanes]
            ma_ref[lanes, lanes] = jnp.dot(cc_ref[lanes, lanes], w, precision=lax.Precision.HIGHEST,
                                           preferred_element_type=F32).astype(BF16)
            mb_ref[lanes, lanes] = jnp.dot(sc_ref[lanes, lanes], w, precision=lax.Precision.HIGHEST,
                                           preferred_element_type=F32).astype(BF16)

    x = u_ref[...].reshape(DFT_N1, DFT_N2, FOURIER_WIDTH)
    xt = jnp.swapaxes(x, 0, 1)

    m1 = m1_ref[...]
    q = []
    for n2 in range(DFT_N2):
        q.append(jnp.dot(m1, xt[n2], preferred_element_type=F32).astype(BF16))
    q = jnp.swapaxes(jnp.stack(q, axis=0), 0, 1)
    nyquist = DFT_N1 // 2
    no_im = jnp.zeros((DFT_N2, FOURIER_WIDTH), BF16)

    half = FOURIER_WIDTH // 2
    lane_halves = (slice(0, half), slice(half, FOURIER_WIDTH))
    direct, mirrored = [], {}
    for start in range(0, nyquist + 1, DFT_K1_GROUP):
        group = range(start, min(start + DFT_K1_GROUP, nyquist + 1))
        xr, xi = [], []
        for k1 in group:
            im = q[nyquist + k1] if k1 % nyquist else no_im
            qk = jnp.concatenate([q[k1], im], axis=0)
            xk = jnp.dot(g2_ref[k1], qk, preferred_element_type=F32)
            xr.append(xk[:DFT_N2].astype(BF16))
            xi.append(xk[DFT_N2:].astype(BF16))
        xr = jnp.concatenate(xr, axis=0)
        xi = jnp.concatenate(xi, axis=0)
        p = jnp.concatenate([jnp.dot(xr[:, lanes], ma_ref[lanes, lanes],
                                     preferred_element_type=F32) for lanes in lane_halves], axis=1)
        m = jnp.concatenate([jnp.dot(xi[:, lanes], mb_ref[lanes, lanes],
                                     preferred_element_type=F32) for lanes in lane_halves], axis=1)
        plus, minus = (p + m).astype(BF16), (p - m).astype(BF16)
        for j, k1 in enumerate(group):
            rows = slice(j * DFT_N2, (j + 1) * DFT_N2)
            if k1 < nyquist:
                direct.append(plus[rows])
            if k1 > 0:
                mirrored[k1] = minus[rows]
    low = jnp.swapaxes(jnp.stack(direct, axis=0), 0, 1)
    high = jnp.swapaxes(jnp.stack([mirrored[k1] for k1 in range(nyquist, 0, -1)], axis=0),
                        0, 1)
    for k2 in range(DFT_N2):
        y_ref[k2 * DFT_N1:k2 * DFT_N1 + nyquist, :] = low[k2]
        y_ref[k2 * DFT_N1 + nyquist:(k2 + 1) * DFT_N1, :] = high[DFT_N2 - 1 - k2]


def _fourier(u, m1, g2, cc, sc, wbd, batch, seq):
    return pl.pallas_call(
        _fourier_kernel,
        grid=(batch,),
        in_specs=[
            pl.BlockSpec((seq, FOURIER_WIDTH), lambda b: (b, 0)),
            _resident((DFT_N1, DFT_N1)),
            _resident((DFT_N1 // 2 + 1, 2 * DFT_N2, 2 * DFT_N2)),
            _resident((FOURIER_WIDTH, FOURIER_WIDTH)),
            _resident((FOURIER_WIDTH, FOURIER_WIDTH)),
            _resident((FOURIER_WIDTH, FOURIER_WIDTH)),
        ],
        out_specs=pl.BlockSpec((seq, FOURIER_WIDTH), lambda b: (b, 0)),
        out_shape=jax.ShapeDtypeStruct((batch * seq, FOURIER_WIDTH), BF16),
        scratch_shapes=[pltpu.VMEM((FOURIER_WIDTH, FOURIER_WIDTH), BF16),
                        pltpu.VMEM((FOURIER_WIDTH, FOURIER_WIDTH), BF16)],
        compiler_params=pltpu.CompilerParams(
            dimension_semantics=("arbitrary",), vmem_limit_bytes=VMEM_LIMIT),
        name="fourier",
    )(u, m1, g2, cc, sc, wbd)


def _attn_mlp_kernel(qt_ref, k_ref, vxt_ref, f_ref, x_ref, wo_ref, g2_ref, wu_ref, wd_ref,
                     g3_ref, o_ref, attn_ref):
    @pl.when(pl.program_id(0) == 0)
    def _():
        attn_ref[...] = jnp.zeros_like(attn_ref)

    def scores(h):
        q = qt_ref[0, h * HEAD_DIM:(h + 1) * HEAD_DIM, :]
        zero = jnp.zeros_like(q)
        g = h // (N_Q_HEADS // N_KV_HEADS)
        q_pad = jnp.concatenate([zero] * g + [q] + [zero] * (N_KV_HEADS - 1 - g), axis=0)
        return jnp.dot(k_ref[...], q_pad, preferred_element_type=F32)

    mlp = {}

    def mix():
        m = (jnp.dot(attn_ref[...], wo_ref[:ATTN_WIDTH, :], preferred_element_type=F32)
             + jnp.dot(f_ref[...], wo_ref[ATTN_WIDTH:, :], preferred_element_type=F32))
        x1 = x_ref[...] + m
        mlp["acc"] = x1
        mlp["h"] = (x1 * _rms_scale(x1) * g2_ref[...]).astype(BF16)

    def up(c):
        z = jnp.dot(mlp["h"], wu_ref[:, c * FF_CHUNK:(c + 1) * FF_CHUNK],
                    preferred_element_type=F32)
        mlp["z"] = jnp.square(jnp.maximum(z, 0.0)).astype(BF16)

    def down(c):
        mlp["acc"] = mlp["acc"] + jnp.dot(mlp["z"], wd_ref[c * FF_CHUNK:(c + 1) * FF_CHUNK, :],
                                         preferred_element_type=F32)

    def down_and_finish(c):
        wd = wd_ref[c * FF_CHUNK:(c + 1) * FF_CHUNK, :]
        for rows in (slice(0, Q_TILE // 2), slice(Q_TILE // 2, Q_TILE)):
            acc = mlp["acc"][rows] + jnp.dot(mlp["z"][rows], wd, preferred_element_type=F32)
            o_ref[rows, :] = acc * _rms_scale(acc) * g3_ref[...]

    units = []
    n_chunks = D_FF // FF_CHUNK
    for c in range(n_chunks):
        units += [functools.partial(up, c),
                  functools.partial(down if c + 1 < n_chunks else down_and_finish, c)]
    assert len(units) == N_Q_HEADS

    s_next = scores(0)
    mix()
    halves = []
    for h in range(N_Q_HEADS):
        s = s_next
        if h + 1 < N_Q_HEADS:
            s_next = scores(h + 1)
        units[h]()
        m = jnp.max(s, axis=0, keepdims=True)
        e = jnp.exp2(s - m).astype(BF16)
        g = h // (N_Q_HEADS // N_KV_HEADS)
        r = jnp.dot(vxt_ref[0, g * V7X_LANES:(g + 1) * V7X_LANES, :], e,
                    preferred_element_type=F32)
        halves.append(r[:HEAD_DIM] / r[HEAD_DIM:])
        if h % 2 == 1:
            c = h // 2
            pair = jnp.concatenate(halves, axis=0)
            attn_ref[:, c * V7X_LANES:(c + 1) * V7X_LANES] = pair.T.astype(BF16)
            halves = []


def _attn_mlp(qt, kx, vxt, four, x2, w_out, g2, w_up, w_down, g3, batch, seq):
    n = x2.shape[0]
    tiles = n // Q_TILE
    per_seq = seq // Q_TILE
    attn_tile = lambda i: jnp.minimum(i, tiles - 1)
    mlp_tile = lambda i: jnp.maximum(i - 1, 0)
    return pl.pallas_call(
        _attn_mlp_kernel,
        grid=(tiles + 1,),
        in_specs=[
            pl.BlockSpec((1, ATTN_WIDTH, Q_TILE),
                         lambda i: (attn_tile(i) // per_seq, 0, attn_tile(i) % per_seq)),
            pl.BlockSpec((seq, KV_WIDTH), lambda i: (attn_tile(i) // per_seq, 0)),
            pl.BlockSpec((1, N_KV_HEADS * V7X_LANES, seq),
                         lambda i: (attn_tile(i) // per_seq, 0, 0)),
            pl.BlockSpec((Q_TILE, FOURIER_WIDTH), lambda i: (mlp_tile(i), 0)),
            pl.BlockSpec((Q_TILE, D_MODEL), lambda i: (mlp_tile(i), 0)),
            _resident((D_MODEL, D_MODEL)),
            _resident((1, D_MODEL)),
            _resident((D_MODEL, D_FF)),
            _resident((D_FF, D_MODEL)),
            _resident((1, D_MODEL)),
        ],
        out_specs=pl.BlockSpec((Q_TILE, D_MODEL), lambda i: (mlp_tile(i), 0)),
        out_shape=jax.ShapeDtypeStruct((n, D_MODEL), F32),
        scratch_shapes=[pltpu.VMEM((Q_TILE, ATTN_WIDTH), BF16)],
        compiler_params=pltpu.CompilerParams(
            dimension_semantics=("arbitrary",), vmem_limit_bytes=FUSED_VMEM_LIMIT),
        name="attn_mlp",
    )(qt, kx, vxt, four, x2, w_out, g2, w_up, w_down, g3)


@functools.lru_cache(maxsize=None)
def _rope_tables(seq):
    rows = seq // GRID_W
    row = np.repeat(np.arange(rows), GRID_W).astype(np.float64)
    col = np.tile(np.arange(GRID_W), rows).astype(np.float64)
    inv_freq = ROPE_THETA ** (-np.arange(0, AXIS_DIM, 2, dtype=np.float64) / AXIS_DIM)
    row_ang = row[:, None] * inv_freq[None, :]
    col_ang = col[:, None] * inv_freq[None, :]
    ang = np.concatenate([row_ang, row_ang, col_ang, col_ang], axis=-1)
    ang = np.concatenate([ang, ang], axis=-1)
    first_half = (np.arange(V7X_LANES) % AXIS_DIM) < (AXIS_DIM // 2)
    sin = np.sin(ang)
    f32 = lambda a: a.astype(np.float32)
    return (f32(np.cos(ang)), f32(np.where(first_half, -sin, 0.0)),
            f32(np.where(first_half, 0.0, sin)))


@functools.lru_cache(maxsize=None)
def _dft_constants(seq):
    assert seq == DFT_N1 * DFT_N2
    k1 = np.arange(DFT_N1, dtype=np.int64)
    a1 = 2.0 * np.pi * ((k1[:, None] * k1[None, :]) % DFT_N1) / DFT_N1
    nyq = DFT_N1 // 2
    m1 = np.concatenate([np.cos(a1[:nyq + 1]), -np.sin(a1[1:nyq])], axis=0)
    n2 = np.arange(DFT_N2, dtype=np.int64)
    a2 = 2.0 * np.pi * ((n2[:, None] * n2[None, :]) % DFT_N2) / DFT_N2
    c2, s2 = np.cos(a2), np.sin(a2)
    r2 = np.block([[c2, s2], [-s2, c2]])
    at = 2.0 * np.pi * (k1[:nyq + 1, None] * n2[None, :]) / seq
    g2 = np.stack([r2 @ np.block([[np.diag(np.cos(a)), np.diag(np.sin(a))],
                                  [-np.diag(np.sin(a)), np.diag(np.cos(a))]]) for a in at])
    c = np.arange(GROUP_DIM, dtype=np.int64)
    angc = 2.0 * np.pi * ((c[:, None] * c[None, :]) % GROUP_DIM) / GROUP_DIM
    scale = 1.0 / np.sqrt(float(seq * GROUP_DIM))
    eye = np.eye(N_GROUPS)
    cc = np.kron(eye, np.cos(angc)) * scale
    sc = np.kron(eye, np.sin(angc)) * scale
    ones_bd = np.kron(np.eye(V7X_LANES // HEAD_DIM), np.ones((HEAD_DIM, HEAD_DIM)))
    f32 = lambda a: a.astype(np.float32)
    return f32(m1), f32(g2), f32(cc), f32(sc), f32(ones_bd)


def kernel(x, mix_norm_g, w_in, q_norm_g, k_norm_g, w_fourier, w_out, mlp_norm_g, w_up,
           w_down, final_norm_g):
    batch, seq, d_model = x.shape
    assert d_model == D_MODEL and seq % Q_TILE == 0 and seq % IN_TILE == 0
    x2 = x.reshape(batch * seq, d_model)

    m1_np, g2_np, cc_np, sc_np, ones_np = _dft_constants(seq)
    m1 = jnp.asarray(m1_np).astype(BF16)
    g2 = jnp.asarray(g2_np).astype(BF16)
    cc, sc = jnp.asarray(cc_np), jnp.asarray(sc_np)
    ones_bd = jnp.asarray(ones_np, dtype=BF16)
    cos_t, sin_a, sin_b = (jnp.asarray(a) for a in _rope_tables(seq))

    row = lambda g: g.reshape(1, -1).astype(F32)
    two_heads = lambda g: jnp.tile(g.astype(F32), V7X_LANES // HEAD_DIM).reshape(1, V7X_LANES)
    wbd = (jnp.eye(N_GROUPS, dtype=F32)[:, None, :, None]
           * w_fourier.astype(F32)[:, :, None, :]).reshape(FOURIER_WIDTH, FOURIER_WIDTH)

    qt, kx, vxt, u, w_out16, w_up16, w_down16 = _in_proj(
        x2, row(mix_norm_g), w_in.astype(F32), ones_bd, two_heads(q_norm_g),
        two_heads(k_norm_g), cos_t, sin_a, sin_b, w_out.astype(F32), w_up.astype(F32),
        w_down.astype(F32), batch, seq)
    four = _fourier(u, m1, g2, cc, sc, wbd, batch, seq)
    out = _attn_mlp(qt, kx, vxt, four, x2, w_out16, row(mlp_norm_g), w_up16, w_down16,
                    row(final_norm_g), batch, seq)
    return out.reshape(batch, seq, d_model)
```

```python
import functools

import numpy as np
import jax
import jax.numpy as jnp
from jax import lax
from jax.experimental import pallas as pl
from jax.experimental.pallas import tpu as pltpu

D_MODEL = 1024
HEAD_DIM = 64
N_Q_HEADS = 8
N_KV_HEADS = 2
ATTN_WIDTH = N_Q_HEADS * HEAD_DIM
KV_WIDTH = N_KV_HEADS * HEAD_DIM
N_GROUPS = 8
GROUP_DIM = 64
FOURIER_WIDTH = N_GROUPS * GROUP_DIM
IN_PROJ_WIDTH = ATTN_WIDTH + 2 * KV_WIDTH + FOURIER_WIDTH
D_FF = 4 * D_MODEL
GRID_W = 64
AXIS_DIM = HEAD_DIM // 2
ROPE_THETA = 10000.0
NORM_EPS = 1e-6
LOG2_E = 1.4426950408889634

V7X_LANES = 128
V7X_VMEM_BYTES = 64 * 1024 * 1024

IN_TILE = 2048
IN_SUB = 256
Q_TILE = 512
FF_CHUNK = 1024
DFT_N1 = 32
DFT_N2 = 64
DFT_K1_GROUP = 9
VMEM_LIMIT = 56 * 1024 * 1024
FUSED_VMEM_LIMIT = 60 * 1024 * 1024

BF16 = jnp.bfloat16
F32 = jnp.float32


def _resident(shape):
    zeros = (0,) * len(shape)
    return pl.BlockSpec(shape, lambda *_: zeros, pipeline_mode=pl.Buffered(1))


def _rms_scale(x):
    return lax.rsqrt(jnp.mean(x * x, axis=-1, keepdims=True) + NORM_EPS)


def _in_proj_kernel(x_ref, g_ref, w_ref, perm_ref, ones_ref, onesk_ref, qga_ref, qgb_ref,
                    kg_ref, cos_ref, sin_ref, sink_ref, wo_ref, wu_ref, wd_ref,
                    qt_ref, k_ref, vxt_ref, u_ref, wo16_ref, wu16_ref, wd16_ref, w16_ref):
    wo16_ref[...] = wo_ref[...].astype(BF16)
    wu16_ref[...] = wu_ref[...].astype(BF16)
    wd16_ref[...] = wd_ref[...].astype(BF16)

    @pl.when(pl.program_id(0) == 0)
    def _():
        n_qk = perm_ref.shape[0]
        w16_ref[:, :n_qk] = jnp.dot(w_ref[:, :n_qk].astype(BF16), perm_ref[...],
                                    preferred_element_type=F32).astype(BF16)
        w16_ref[:, n_qk:] = w_ref[:, n_qk:].astype(BF16)

    g = g_ref[...]
    ones = ones_ref[...]
    ones_k = onesk_ref[...]
    qga = qga_ref[...]
    qgb = qgb_ref[...]
    kg = kg_ref[...]
    w = w16_ref[...]

    def project(i):
        x = x_ref[i * IN_SUB:(i + 1) * IN_SUB, :]
        h = x * _rms_scale(x) * g
        return jnp.dot(h.astype(BF16), w, preferred_element_type=F32)

    def finish(i, proj):
        rows = slice(i * IN_SUB, (i + 1) * IN_SUB)
        cos = cos_ref[rows, :]
        sin = sin_ref[rows, :]

        def inv_rms(squares, ones):
            ss = jnp.dot(squares.astype(BF16), ones, preferred_element_type=F32)
            return lax.rsqrt(ss * (1.0 / HEAD_DIM) + NORM_EPS)

        half = ATTN_WIDTH // 2
        for c in range(half // V7X_LANES):
            la = slice(c * V7X_LANES, (c + 1) * V7X_LANES)
            lb = slice(half + c * V7X_LANES, half + (c + 1) * V7X_LANES)
            a, b = proj[:, la], proj[:, lb]
            inv = inv_rms(a * a + b * b, ones) * (HEAD_DIM ** -0.5 * LOG2_E)
            ya, yb = a * inv * qga, b * inv * qgb
            qt_ref[0, la, rows] = (ya * cos - yb * sin).T.astype(BF16)
            qt_ref[0, lb, rows] = (yb * cos + ya * sin).T.astype(BF16)

        k = proj[:, ATTN_WIDTH:ATTN_WIDTH + KV_WIDTH]
        yk = k * inv_rms(k * k, ones_k) * kg
        k_ref[rows, :] = (yk * cos + pltpu.roll(yk, HEAD_DIM, 1) * sink_ref[rows, :]).astype(BF16)

        vt = proj[:, ATTN_WIDTH + KV_WIDTH:ATTN_WIDTH + 2 * KV_WIDTH].T.astype(BF16)
        one = jnp.ones((HEAD_DIM, vt.shape[1]), BF16)
        for g in range(N_KV_HEADS):
            vxt_ref[0, g * V7X_LANES:g * V7X_LANES + HEAD_DIM, rows] = (
                vt[g * HEAD_DIM:(g + 1) * HEAD_DIM])
            vxt_ref[0, g * V7X_LANES + HEAD_DIM:(g + 1) * V7X_LANES, rows] = one

        u_ref[rows, :] = proj[:, ATTN_WIDTH + 2 * KV_WIDTH:].astype(BF16)

    n_sub = IN_TILE // IN_SUB
    nxt = project(0)
    for i in range(n_sub):
        cur = nxt
        if i + 1 < n_sub:
            nxt = project(i + 1)
        finish(i, cur)


def _in_proj(x2, mix_g, w_in, perm, ones_q, ones_k, qga, qgb, kg, cos_t, sin_t, sin_k,
             w_out, w_up, w_down, batch, seq):
    n = x2.shape[0]
    steps = n // IN_TILE
    tiles_per_seq = seq // IN_TILE
    tok = lambda i: (i, 0)
    pos = lambda i: (i % tiles_per_seq, 0)
    chan_major = lambda i: (i // tiles_per_seq, 0, i % tiles_per_seq)
    slab = lambda w: pl.BlockSpec((w.shape[0] // steps, w.shape[1]), tok)
    bf16_like = lambda w: jax.ShapeDtypeStruct(w.shape, BF16)
    return pl.pallas_call(
        _in_proj_kernel,
        grid=(steps,),
        in_specs=[
            pl.BlockSpec((IN_TILE, D_MODEL), tok),
            _resident((1, D_MODEL)),
            _resident((D_MODEL, IN_PROJ_WIDTH)),
            _resident(perm.shape),
            _resident((V7X_LANES, V7X_LANES)),
            _resident((V7X_LANES, V7X_LANES)),
            _resident((1, V7X_LANES)),
            _resident((1, V7X_LANES)),
            _resident((1, V7X_LANES)),
            pl.BlockSpec((IN_TILE, V7X_LANES), pos),
            pl.BlockSpec((IN_TILE, V7X_LANES), pos),
            pl.BlockSpec((IN_TILE, V7X_LANES), pos),
            slab(w_out), slab(w_up), slab(w_down),
        ],
        out_specs=[
            pl.BlockSpec((1, ATTN_WIDTH, IN_TILE), chan_major),
            pl.BlockSpec((IN_TILE, KV_WIDTH), tok),
            pl.BlockSpec((1, N_KV_HEADS * V7X_LANES, IN_TILE), chan_major),
            pl.BlockSpec((IN_TILE, FOURIER_WIDTH), tok),
            slab(w_out), slab(w_up), slab(w_down),
        ],
        out_shape=[
            jax.ShapeDtypeStruct((batch, ATTN_WIDTH, seq), BF16),
            jax.ShapeDtypeStruct((n, KV_WIDTH), BF16),
            jax.ShapeDtypeStruct((batch, N_KV_HEADS * V7X_LANES, seq), BF16),
            jax.ShapeDtypeStruct((n, FOURIER_WIDTH), BF16),
            bf16_like(w_out), bf16_like(w_up), bf16_like(w_down),
        ],
        scratch_shapes=[pltpu.VMEM((D_MODEL, IN_PROJ_WIDTH), BF16)],
        compiler_params=pltpu.CompilerParams(
            dimension_semantics=("arbitrary",), vmem_limit_bytes=VMEM_LIMIT),
        name="in_proj",
    )(x2, mix_g, w_in, perm, ones_q, ones_k, qga, qgb, kg, cos_t, sin_t, sin_k,
      w_out, w_up, w_down)


def _fourier_kernel(u_ref, m1_ref, g2_ref, cc_ref, sc_ref, wbd_ref, y_ref, ma_ref, mb_ref):
    @pl.when(pl.program_id(0) == 0)
    def _():
        for lanes in (slice(0, FOURIER_WIDTH // 2), slice(FOURIER_WIDTH // 2, FOURIER_WIDTH)):
            w = wbd_ref[lanes, lanes]
            ma_ref[lanes, lanes] = jnp.dot(cc_ref[lanes, lanes], w, precision=lax.Precision.HIGHEST,
                                           preferred_element_type=F32).astype(BF16)
            mb_ref[lanes, lanes] = jnp.dot(sc_ref[lanes, lanes], w, precision=lax.Precision.HIGHEST,
                                           preferred_element_type=F32).astype(BF16)

    x = u_ref[...].reshape(DFT_N1, DFT_N2, FOURIER_WIDTH)
    xt = jnp.swapaxes(x, 0, 1)

    m1 = m1_ref[...]
    q = []
    for n2 in range(DFT_N2):
        q.append(jnp.dot(m1, xt[n2], preferred_element_type=F32).astype(BF16))
    q = jnp.swapaxes(jnp.stack(q, axis=0), 0, 1)
    nyquist = DFT_N1 // 2
    no_im = jnp.zeros((DFT_N2, FOURIER_WIDTH), BF16)

    half = FOURIER_WIDTH // 2
    lane_halves = (slice(0, half), slice(half, FOURIER_WIDTH))
    direct, mirrored = [], {}
    for start in range(0, nyquist + 1, DFT_K1_GROUP):
        group = range(start, min(start + DFT_K1_GROUP, nyquist + 1))
        xr, xi = [], []
        for k1 in group:
            im = q[nyquist + k1] if k1 % nyquist else no_im
            qk = jnp.concatenate([q[k1], im], axis=0)
            xk = jnp.dot(g2_ref[k1], qk, preferred_element_type=F32)
            xr.append(xk[:DFT_N2].astype(BF16))
            xi.append(xk[DFT_N2:].astype(BF16))
        xr = jnp.concatenate(xr, axis=0)
        xi = jnp.concatenate(xi, axis=0)
        p = jnp.concatenate([jnp.dot(xr[:, lanes], ma_ref[lanes, lanes],
                                     preferred_element_type=F32) for lanes in lane_halves], axis=1)
        m = jnp.concatenate([jnp.dot(xi[:, lanes], mb_ref[lanes, lanes],
                                     preferred_element_type=F32) for lanes in lane_halves], axis=1)
        plus, minus = (p + m).astype(BF16), (p - m).astype(BF16)
        for j, k1 in enumerate(group):
            rows = slice(j * DFT_N2, (j + 1) * DFT_N2)
            if k1 < nyquist:
                direct.append(plus[rows])
            if k1 > 0:
                mirrored[k1] = minus[rows]
    low = jnp.swapaxes(jnp.stack(direct, axis=0), 0, 1)
    high = jnp.swapaxes(jnp.stack([mirrored[k1] for k1 in range(nyquist, 0, -1)], axis=0),
                        0, 1)
    for k2 in range(DFT_N2):
        y_ref[k2 * DFT_N1:k2 * DFT_N1 + nyquist, :] = low[k2]
        y_ref[k2 * DFT_N1 + nyquist:(k2 + 1) * DFT_N1, :] = high[DFT_N2 - 1 - k2]


def _fourier(u, m1, g2, cc, sc, wbd, batch, seq):
    return pl.pallas_call(
        _fourier_kernel,
        grid=(batch,),
        in_specs=[
            pl.BlockSpec((seq, FOURIER_WIDTH), lambda b: (b, 0)),
            _resident((DFT_N1, DFT_N1)),
            _resident((DFT_N1 // 2 + 1, 2 * DFT_N2, 2 * DFT_N2)),
            _resident((FOURIER_WIDTH, FOURIER_WIDTH)),
            _resident((FOURIER_WIDTH, FOURIER_WIDTH)),
            _resident((FOURIER_WIDTH, FOURIER_WIDTH)),
        ],
        out_specs=pl.BlockSpec((seq, FOURIER_WIDTH), lambda b: (b, 0)),
        out_shape=jax.ShapeDtypeStruct((batch * seq, FOURIER_WIDTH), BF16),
        scratch_shapes=[pltpu.VMEM((FOURIER_WIDTH, FOURIER_WIDTH), BF16),
                        pltpu.VMEM((FOURIER_WIDTH, FOURIER_WIDTH), BF16)],
        compiler_params=pltpu.CompilerParams(
            dimension_semantics=("arbitrary",), vmem_limit_bytes=VMEM_LIMIT),
        name="fourier",
    )(u, m1, g2, cc, sc, wbd)


def _attn_mlp_kernel(qt_ref, k_ref, vxt_ref, f_ref, x_ref, wo_ref, g2_ref, wu_ref, wd_ref,
                     g3_ref, o_ref, attn_ref):
    @pl.when(pl.program_id(0) == 0)
    def _():
        attn_ref[...] = jnp.zeros_like(attn_ref)

    def scores(h):
        hw = HEAD_DIM // 2
        first = qt_ref[0, h * hw:(h + 1) * hw, :]
        second = qt_ref[0, ATTN_WIDTH // 2 + h * hw:ATTN_WIDTH // 2 + (h + 1) * hw, :]
        zero = jnp.zeros_like(first)
        g = h // (N_Q_HEADS // N_KV_HEADS)
        pad = lambda part: [zero] * g + [part] + [zero] * (N_KV_HEADS - 1 - g)
        q_pad = jnp.concatenate(pad(first) + pad(second), axis=0)
        return jnp.dot(k_ref[...], q_pad, preferred_element_type=F32)

    mlp = {}

    def mix():
        m = (jnp.dot(attn_ref[...], wo_ref[:ATTN_WIDTH, :], preferred_element_type=F32)
             + jnp.dot(f_ref[...], wo_ref[ATTN_WIDTH:, :], preferred_element_type=F32))
        x1 = x_ref[...] + m
        mlp["acc"] = x1
        mlp["h"] = (x1 * _rms_scale(x1) * g2_ref[...]).astype(BF16)

    def up(c):
        z = jnp.dot(mlp["h"], wu_ref[:, c * FF_CHUNK:(c + 1) * FF_CHUNK],
                    preferred_element_type=F32)
        mlp["z"] = jnp.square(jnp.maximum(z, 0.0)).astype(BF16)

    def down(c):
        mlp["acc"] = mlp["acc"] + jnp.dot(mlp["z"], wd_ref[c * FF_CHUNK:(c + 1) * FF_CHUNK, :],
                                         preferred_element_type=F32)

    def down_and_finish(c):
        wd = wd_ref[c * FF_CHUNK:(c + 1) * FF_CHUNK, :]
        for rows in (slice(0, Q_TILE // 2), slice(Q_TILE // 2, Q_TILE)):
            acc = mlp["acc"][rows] + jnp.dot(mlp["z"][rows], wd, preferred_element_type=F32)
            o_ref[rows, :] = acc * _rms_scale(acc) * g3_ref[...]

    units = []
    n_chunks = D_FF // FF_CHUNK
    for c in range(n_chunks):
        units += [functools.partial(up, c),
                  functools.partial(down if c + 1 < n_chunks else down_and_finish, c)]
    assert len(units) == N_Q_HEADS

    s_next = scores(0)
    mix()
    halves = []
    for h in range(N_Q_HEADS):
        s = s_next
        if h + 1 < N_Q_HEADS:
            s_next = scores(h + 1)
        units[h]()
        m = jnp.max(s, axis=0, keepdims=True)
        e = jnp.exp2(s - m).astype(BF16)
        g = h // (N_Q_HEADS // N_KV_HEADS)
        r = jnp.dot(vxt_ref[0, g * V7X_LANES:(g + 1) * V7X_LANES, :], e,
                    preferred_element_type=F32)
        halves.append(r[:HEAD_DIM] / r[HEAD_DIM:])
        if h % 2 == 1:
            c = h // 2
            pair = jnp.concatenate(halves, axis=0)
            attn_ref[:, c * V7X_LANES:(c + 1) * V7X_LANES] = pair.T.astype(BF16)
            halves = []


def _attn_mlp(qt, kx, vxt, four, x2, w_out, g2, w_up, w_down, g3, batch, seq):
    n = x2.shape[0]
    tiles = n // Q_TILE
    per_seq = seq // Q_TILE
    attn_tile = lambda i: jnp.minimum(i, tiles - 1)
    mlp_tile = lambda i: jnp.maximum(i - 1, 0)
    return pl.pallas_call(
        _attn_mlp_kernel,
        grid=(tiles + 1,),
        in_specs=[
            pl.BlockSpec((1, ATTN_WIDTH, Q_TILE),
                         lambda i: (attn_tile(i) // per_seq, 0, attn_tile(i) % per_seq)),
            pl.BlockSpec((seq, KV_WIDTH), lambda i: (attn_tile(i) // per_seq, 0)),
            pl.BlockSpec((1, N_KV_HEADS * V7X_LANES, seq),
                         lambda i: (attn_tile(i) // per_seq, 0, 0)),
            pl.BlockSpec((Q_TILE, FOURIER_WIDTH), lambda i: (mlp_tile(i), 0)),
            pl.BlockSpec((Q_TILE, D_MODEL), lambda i: (mlp_tile(i), 0)),
            _resident((D_MODEL, D_MODEL)),
            _resident((1, D_MODEL)),
            _resident((D_MODEL, D_FF)),
            _resident((D_FF, D_MODEL)),
            _resident((1, D_MODEL)),
        ],
        out_specs=pl.BlockSpec((Q_TILE, D_MODEL), lambda i: (mlp_tile(i), 0)),
        out_shape=jax.ShapeDtypeStruct((n, D_MODEL), F32),
        scratch_shapes=[pltpu.VMEM((Q_TILE, ATTN_WIDTH), BF16)],
        compiler_params=pltpu.CompilerParams(
            dimension_semantics=("arbitrary",), vmem_limit_bytes=FUSED_VMEM_LIMIT),
        name="attn_mlp",
    )(qt, kx, vxt, four, x2, w_out, g2, w_up, w_down, g3)


@functools.lru_cache(maxsize=None)
def _qk_layout():
    quarter = AXIS_DIM // 2
    firsts = np.concatenate([np.arange(quarter), AXIS_DIM + np.arange(quarter)])
    seconds = firsts + quarter
    q_cols = np.concatenate([HEAD_DIM * h + part for part in (firsts, seconds)
                             for h in range(N_Q_HEADS)])
    k_cols = ATTN_WIDTH + np.concatenate([HEAD_DIM * h + part for part in (firsts, seconds)
                                          for h in range(N_KV_HEADS)])
    old = np.concatenate([q_cols, k_cols])
    perm = np.zeros((old.size, old.size), np.float32)
    perm[old, np.arange(old.size)] = 1.0
    q_head = np.arange(V7X_LANES) // AXIS_DIM
    k_head = (np.arange(V7X_LANES) // AXIS_DIM) % N_KV_HEADS
    same = lambda head: (head[:, None] == head[None, :]).astype(np.float32)
    return perm, same(q_head), same(k_head)


def _lane_gains(g, copies_first, copies_second=0):
    quarter = AXIS_DIM // 2
    g = g.astype(F32)
    first = jnp.concatenate([g[:quarter], g[AXIS_DIM:AXIS_DIM + quarter]])
    second = jnp.concatenate([g[quarter:AXIS_DIM], g[AXIS_DIM + quarter:]])
    return jnp.concatenate([first] * copies_first + [second] * copies_second).reshape(1, V7X_LANES)


@functools.lru_cache(maxsize=None)
def _rope_tables(seq):
    rows = seq // GRID_W
    row = np.repeat(np.arange(rows), GRID_W).astype(np.float64)
    col = np.tile(np.arange(GRID_W), rows).astype(np.float64)
    inv_freq = ROPE_THETA ** (-np.arange(0, AXIS_DIM, 2, dtype=np.float64) / AXIS_DIM)
    ang = np.concatenate([row[:, None] * inv_freq[None, :], col[:, None] * inv_freq[None, :]],
                         axis=-1)
    ang = np.tile(ang, (1, V7X_LANES // AXIS_DIM))
    sin = np.sin(ang)
    k_first = np.arange(V7X_LANES) < HEAD_DIM
    f32 = lambda a: a.astype(np.float32)
    return f32(np.cos(ang)), f32(sin), f32(np.where(k_first, -sin, sin))


@functools.lru_cache(maxsize=None)
def _dft_constants(seq):
    assert seq == DFT_N1 * DFT_N2
    k1 = np.arange(DFT_N1, dtype=np.int64)
    a1 = 2.0 * np.pi * ((k1[:, None] * k1[None, :]) % DFT_N1) / DFT_N1
    nyq = DFT_N1 // 2
    m1 = np.concatenate([np.cos(a1[:nyq + 1]), -np.sin(a1[1:nyq])], axis=0)
    n2 = np.arange(DFT_N2, dtype=np.int64)
    a2 = 2.0 * np.pi * ((n2[:, None] * n2[None, :]) % DFT_N2) / DFT_N2
    c2, s2 = np.cos(a2), np.sin(a2)
    r2 = np.block([[c2, s2], [-s2, c2]])
    at = 2.0 * np.pi * (k1[:nyq + 1, None] * n2[None, :]) / seq
    g2 = np.stack([r2 @ np.block([[np.diag(np.cos(a)), np.diag(np.sin(a))],
                                  [-np.diag(np.sin(a)), np.diag(np.cos(a))]]) for a in at])
    c = np.arange(GROUP_DIM, dtype=np.int64)
    angc = 2.0 * np.pi * ((c[:, None] * c[None, :]) % GROUP_DIM) / GROUP_DIM
    scale = 1.0 / np.sqrt(float(seq * GROUP_DIM))
    eye = np.eye(N_GROUPS)
    cc = np.kron(eye, np.cos(angc)) * scale
    sc = np.kron(eye, np.sin(angc)) * scale
    f32 = lambda a: a.astype(np.float32)
    return f32(m1), f32(g2), f32(cc), f32(sc)


def kernel(x, mix_norm_g, w_in, q_norm_g, k_norm_g, w_fourier, w_out, mlp_norm_g, w_up,
           w_down, final_norm_g):
    batch, seq, d_model = x.shape
    assert d_model == D_MODEL and seq % Q_TILE == 0 and seq % IN_TILE == 0
    x2 = x.reshape(batch * seq, d_model)

    m1_np, g2_np, cc_np, sc_np = _dft_constants(seq)
    m1 = jnp.asarray(m1_np).astype(BF16)
    g2 = jnp.asarray(g2_np).astype(BF16)
    cc, sc = jnp.asarray(cc_np), jnp.asarray(sc_np)
    cos_t, sin_t, sin_k = (jnp.asarray(a) for a in _rope_tables(seq))

    row = lambda g: g.reshape(1, -1).astype(F32)
    wbd = (jnp.eye(N_GROUPS, dtype=F32)[:, None, :, None]
           * w_fourier.astype(F32)[:, :, None, :]).reshape(FOURIER_WIDTH, FOURIER_WIDTH)

    perm, same_q, same_k = (jnp.asarray(a, dtype=BF16) for a in _qk_layout())
    per_block = V7X_LANES // AXIS_DIM
    qga = _lane_gains(q_norm_g, per_block)
    qgb = _lane_gains(q_norm_g, 0, per_block)
    kg = _lane_gains(k_norm_g, N_KV_HEADS, N_KV_HEADS)

    qt, kx, vxt, u, w_out16, w_up16, w_down16 = _in_proj(
        x2, row(mix_norm_g), w_in.astype(F32), perm, same_q, same_k, qga, qgb, kg,
        cos_t, sin_t, sin_k, w_out.astype(F32), w_up.astype(F32), w_down.astype(F32),
        batch, seq)
    four = _fourier(u, m1, g2, cc, sc, wbd, batch, seq)
    out = _attn_mlp(qt, kx, vxt, four, x2, w_out16, row(mlp_norm_g), w_up16, w_down16,
                    row(final_norm_g), batch, seq)
    return out.reshape(batch, seq, d_model)
```
